```python
import math
import jax, jax.numpy as jnp
from jax import lax
import numpy as np

D_MODEL = 1024
BATCH = 8
SEQ = 2048
DEPTH = 2
DEC_BATCH = 32
DEC_SEQ = 8
PAST_LEN = 16384
PAGE_SIZE = 128

D_MIX = D_MODEL
D_ATTN = D_MIX // 2
D_POOL = D_MIX - D_ATTN
HEAD_DIM = 64
N_HEADS_A = D_ATTN // HEAD_DIM
POOL_WINDOWS = (2, 4, 8, 16)
N_POOL_GROUPS = len(POOL_WINDOWS)
POOL_GROUP_DIM = D_POOL // N_POOL_GROUPS
POOL_STATE_LEN = max(POOL_WINDOWS) - 1
D_IN = 4 * D_ATTN + 2 * D_POOL
SPLITS = (D_ATTN, 2 * D_ATTN, 3 * D_ATTN, 4 * D_ATTN, 4 * D_ATTN + D_POOL)
Q_BLOCK = 128
SB_SCALE = 1.0 / math.sqrt(HEAD_DIM)
SB_BIAS_LOG2_BASE = 7
RMS_EPS = 1e-6

kernel_name = "hymba_stickbreak_pool_step"


def _rmsnorm(x, g):
    xf = x.astype(jnp.float32)
    y = xf * lax.rsqrt(jnp.mean(xf * xf, axis=-1, keepdims=True) + RMS_EPS)
    return (y * g.astype(jnp.float32)).astype(x.dtype)


def _project(h, w_in):
    p = jnp.einsum("btd,de->bte", h, w_in)
    b, t = p.shape[:2]
    q, k, v, gate_a, u, gate_b = jnp.split(p, SPLITS, axis=-1)
    heads = lambda z: z.reshape(b, t, N_HEADS_A, HEAD_DIM)
    return heads(q), heads(k), heads(v), gate_a, u, gate_b


def _stick_breaking(q, k, v, bias, q_pos, k_pos):
    z = jnp.einsum("bqhd,bkhd->bhqk", q.astype(jnp.float32), k.astype(jnp.float32)) * SB_SCALE
    z = z + bias.astype(jnp.float32)[None, :, None, None]
    mask = (k_pos[None, :] < q_pos[:, None])[None, None]
    log_keep = jnp.where(mask, -jax.nn.softplus(z), 0.0)
    log_between = lax.cumsum(log_keep, axis=3, reverse=True) - log_keep
    weights = jnp.where(mask, jnp.exp(jax.nn.log_sigmoid(z) + log_between), 0.0)
    out = jnp.einsum("bhqk,bkhd->bqhd", weights, v.astype(jnp.float32))
    return out.astype(q.dtype)


def _attend_prompt(q, k, v, bias):
    b, t = q.shape[:2]
    n_blk = t // Q_BLOCK
    pos = jnp.arange(t, dtype=jnp.int32)
    q_blocks = q.reshape(b, n_blk, Q_BLOCK, N_HEADS_A, HEAD_DIM).transpose(1, 0, 2, 3, 4)
    pos_blocks = pos.reshape(n_blk, Q_BLOCK)
    out = lax.map(lambda qp: _stick_breaking(qp[0], k, v, bias, qp[1], pos), (q_blocks, pos_blocks))
    return out.transpose(1, 0, 2, 3, 4).reshape(b, t, D_ATTN)


def _attend_sample(q, k_new, v_new, bias, k_pages, v_pages, page_table):
    b, t = q.shape[:2]
    past = page_table.shape[1] * PAGE_SIZE
    k_past = k_pages[page_table].reshape(b, past, N_HEADS_A, HEAD_DIM)
    v_past = v_pages[page_table].reshape(b, past, N_HEADS_A, HEAD_DIM)
    k_all = jnp.concatenate([k_past.astype(k_new.dtype), k_new], axis=1)
    v_all = jnp.concatenate([v_past.astype(v_new.dtype), v_new], axis=1)
    k_pos = jnp.arange(past + t, dtype=jnp.int32)
    q_pos = past + jnp.arange(t, dtype=jnp.int32)
    return _stick_breaking(q, k_all, v_all, bias, q_pos, k_pos).reshape(b, t, D_ATTN)


def _pool_mixer(u, prefix, pos, w_pool, scale):
    t = u.shape[1]
    seq = jnp.concatenate([prefix.astype(u.dtype), u], axis=1)
    seq_f = seq.astype(jnp.float32)
    cs = jnp.cumsum(seq_f, axis=1)
    cs = jnp.concatenate([jnp.zeros_like(cs[:, :1]), cs], axis=1)
    end = POOL_STATE_LEN + 1
    u_f = seq_f[:, POOL_STATE_LEN:]
    outs = []
    for g, w in enumerate(POOL_WINDOWS):
        sl = slice(g * POOL_GROUP_DIM, (g + 1) * POOL_GROUP_DIM)
        win_sum = cs[:, end:end + t, sl] - cs[:, end - w:end - w + t, sl]
        count = jnp.minimum(pos + 1, w).astype(jnp.float32)[None, :, None]
        diff = win_sum / count - u_f[:, :, sl]
        outs.append(jnp.einsum("btc,ce->bte", diff, w_pool[g].astype(jnp.float32)))
    y = jnp.concatenate(outs, axis=-1) * scale.astype(jnp.float32)
    return y.astype(u.dtype), seq[:, -POOL_STATE_LEN:]


def _merge(x, a, gate_a, b, gate_b, w_out):
    mix = jnp.concatenate([a * jax.nn.silu(gate_a), b * jax.nn.silu(gate_b)], axis=-1)
    return x + jnp.einsum("bte,ed->btd", mix, w_out)


def setup_inputs(seed: int = 0) -> dict:
    key = jax.random.key(seed)
    ks = jax.random.split(key, 13)
    n_pages = PAST_LEN // PAGE_SIZE
    n_used = DEC_BATCH * n_pages
    n_pool = (n_used * 5) // 4
    f32 = jnp.float32
    x_prompt = jax.random.normal(ks[0], (BATCH, SEQ, D_MODEL), f32)
    x_sample = jax.random.normal(ks[1], (DEC_BATCH, DEC_SEQ, D_MODEL), f32)
    cache_k = jax.random.normal(ks[2], (DEPTH, n_pool, PAGE_SIZE, N_HEADS_A, HEAD_DIM), f32)
    cache_v = jax.random.normal(ks[3], (DEPTH, n_pool, PAGE_SIZE, N_HEADS_A, HEAD_DIM), f32)
    state_pool = jax.random.normal(ks[4], (DEPTH, DEC_BATCH, POOL_STATE_LEN, D_POOL), f32)
    page_table = jax.random.permutation(ks[5], n_pool)[:n_used].reshape(DEC_BATCH, n_pages).astype(jnp.int32)
    norm_g = 1.0 + 0.02 * jax.random.normal(ks[6], (DEPTH, D_MODEL), f32)
    w_in = jax.random.normal(ks[7], (DEPTH, D_MODEL, D_IN), f32) * D_MODEL ** -0.5
    reach_log = (SB_BIAS_LOG2_BASE + jnp.arange(N_HEADS_A, dtype=f32)) * math.log(2.0)
    sb_bias = -(reach_log + 0.5)[None, :] + 0.1 * jax.random.normal(ks[12], (DEPTH, N_HEADS_A), f32)
    w_pool = jax.random.normal(ks[8], (DEPTH, N_POOL_GROUPS, POOL_GROUP_DIM, POOL_GROUP_DIM), f32) * POOL_GROUP_DIM ** -0.5
    pool_scale = 1.0 + 0.1 * jax.random.normal(ks[9], (DEPTH, D_POOL), f32)
    w_out = jax.random.normal(ks[10], (DEPTH, D_MIX, D_MODEL), f32) * D_MIX ** -0.5
    final_g = 1.0 + 0.02 * jax.random.normal(ks[11], (D_MODEL,), f32)
    return {"x_prompt": x_prompt, "x_sample": x_sample, "cache_k": cache_k, "cache_v": cache_v,
            "state_pool": state_pool, "page_table": page_table, "norm_g": norm_g, "w_in": w_in,
            "sb_bias": sb_bias, "w_pool": w_pool, "pool_scale": pool_scale, "w_out": w_out,
            "final_g": final_g}


def reference(x_prompt, x_sample, cache_k, cache_v, state_pool, page_table, norm_g, w_in, sb_bias, w_pool, pool_scale, w_out, final_g):
    seq_len = x_prompt.shape[1]
    dec_len = x_sample.shape[1]
    past = page_table.shape[1] * PAGE_SIZE
    pos_prompt = jnp.arange(seq_len, dtype=jnp.int32)
    pos_sample = past + jnp.arange(dec_len, dtype=jnp.int32)

    h_p = x_prompt
    zero_prefix = jnp.zeros((x_prompt.shape[0], POOL_STATE_LEN, D_POOL), x_prompt.dtype)
    k_prompt, v_prompt, pool_prompt = [], [], []
    for l in range(DEPTH):
        q, k, v, gate_a, u, gate_b = _project(_rmsnorm(h_p, norm_g[l]), w_in[l])
        a = _attend_prompt(q, k, v, sb_bias[l])
        b, s_new = _pool_mixer(u, zero_prefix, pos_prompt, w_pool[l], pool_scale[l])
        h_p = _merge(h_p, a, gate_a, b, gate_b, w_out[l])
        k_prompt.append(k)
        v_prompt.append(v)
        pool_prompt.append(s_new)
    y_prompt = _rmsnorm(h_p, final_g)

    h_s = x_sample
    k_sample, v_sample, pool_sample = [], [], []
    for l in range(DEPTH):
        q, k, v, gate_a, u, gate_b = _project(_rmsnorm(h_s, norm_g[l]), w_in[l])
        a = _attend_sample(q, k, v, sb_bias[l], cache_k[l], cache_v[l], page_table)
        b, s_new = _pool_mixer(u, state_pool[l], pos_sample, w_pool[l], pool_scale[l])
        h_s = _merge(h_s, a, gate_a, b, gate_b, w_out[l])
        k_sample.append(k)
        v_sample.append(v)
        pool_sample.append(s_new)
    y_sample = _rmsnorm(h_s, final_g)

    return (y_prompt, y_sample, jnp.stack(k_prompt), jnp.stack(v_prompt), jnp.stack(pool_prompt),
            jnp.stack(k_sample), jnp.stack(v_sample), jnp.stack(pool_sample))
```

```python
import functools
import math

import jax
import jax.numpy as jnp
from jax import lax
from jax.experimental import pallas as pl
from jax.experimental.pallas import tpu as pltpu

F32 = jnp.float32
BF16 = jnp.bfloat16

PAGE_SIZE = 128
HEAD_DIM = 64
POOL_WINDOWS = (2, 4, 8, 16)
POOL_GROUP_DIM = 128
POOL_STATE_LEN = max(POOL_WINDOWS) - 1
HALO = POOL_STATE_LEN + 1
RMS_EPS = 1e-6
SB_SCALE = 1.0 / math.sqrt(HEAD_DIM)

VMEM_LIMIT_BYTES = 56 * 1024 * 1024

_NT = (((1,), (1,)), ((), ()))


def _params(*sem):
    return pltpu.CompilerParams(dimension_semantics=sem, vmem_limit_bytes=VMEM_LIMIT_BYTES)


def _norm_proj_kernel(x_ref, g_ref, w_ref, wkvt_ref, q_ref, k_ref, v_ref, kb_ref, vb_ref,
                      ga_ref, u_ref, gb_ref, *, d_attn, d_pool, prompt):
    x = x_ref[0]
    ms = jnp.mean(x * x, axis=-1, keepdims=True)
    h = (x * lax.rsqrt(ms + RMS_EPS) * g_ref[...]).astype(BF16)

    def proj(lo, width):
        return jnp.dot(h, w_ref[:, lo:lo + width], preferred_element_type=F32)

    q_ref[0] = (proj(0, d_attn) * SB_SCALE).astype(BF16)
    kvt = lax.dot_general(wkvt_ref[...], h, _NT, preferred_element_type=F32)
    kb_ref[0] = kvt[:d_attn].astype(BF16)
    if prompt:
        k_ref[0] = kvt[:d_attn]
        v_ref[0] = kvt[d_attn:]
        vb_ref[0] = proj(2 * d_attn, d_attn).astype(BF16)
    else:
        k_ref[0] = proj(d_attn, d_attn)
        v_ref[0] = proj(2 * d_attn, d_attn)
        vb_ref[0] = kvt[d_attn:].astype(BF16)
    ga_ref[0] = proj(3 * d_attn, d_attn)
    u_ref[0] = proj(4 * d_attn, d_pool)
    gb_ref[0] = proj(4 * d_attn + d_pool, d_pool)


def _norm_proj(x, g, w_bf16, wkvt_bf16, d_attn, d_pool, tm, prompt):
    b, t, d = x.shape
    d_in = w_bf16.shape[1]
    tok = lambda width: pl.BlockSpec((1, tm, width), lambda bi, i: (bi, i, 0))
    feat = pl.BlockSpec((1, d_attn, tm), lambda bi, i: (bi, 0, i))
    tok_shape = lambda width, dt: jax.ShapeDtypeStruct((b, t, width), dt)
    feat_shape = lambda dt: jax.ShapeDtypeStruct((b, d_attn, t), dt)
    if prompt:
        kv_specs = [feat, feat, feat, tok(d_attn)]
        kv_shapes = [feat_shape(F32), feat_shape(F32), feat_shape(BF16), tok_shape(d_attn, BF16)]
    else:
        kv_specs = [tok(d_attn), tok(d_attn), feat, feat]
        kv_shapes = [tok_shape(d_attn, F32), tok_shape(d_attn, F32), feat_shape(BF16), feat_shape(BF16)]
    return pl.pallas_call(
        functools.partial(_norm_proj_kernel, d_attn=d_attn, d_pool=d_pool, prompt=prompt),
        grid=(b, t // tm),
        in_specs=[
            tok(d),
            pl.BlockSpec((1, d), lambda bi, i: (0, 0)),
            pl.BlockSpec((d, d_in), lambda bi, i: (0, 0)),
            pl.BlockSpec((2 * d_attn, d), lambda bi, i: (0, 0)),
        ],
        out_specs=[tok(d_attn)] + kv_specs + [tok(d_attn), tok(d_pool), tok(d_pool)],
        out_shape=[tok_shape(d_attn, BF16)] + kv_shapes
        + [tok_shape(d_attn, F32), tok_shape(d_pool, F32), tok_shape(d_pool, F32)],
        compiler_params=_params("parallel", "parallel"),
        name="norm_proj",
    )(x, g.reshape(1, d), w_bf16, wkvt_bf16)


def _neg_softplus(z):
    return -(jnp.maximum(z, 0.0) + jnp.log1p(jnp.exp(-jnp.abs(z))))


def _suffix_matrix(n):
    j = lax.broadcasted_iota(jnp.int32, (n, n), 0)
    s = lax.broadcasted_iota(jnp.int32, (n, n), 1)
    return jnp.where(j >= s, 1.0, 0.0).astype(BF16)


def _suffix_sums(x, suffix):
    hi = x.astype(BF16)
    lo = (x - hi.astype(F32)).astype(BF16)
    return (jnp.dot(hi, suffix, preferred_element_type=F32)
            + jnp.dot(lo, suffix, preferred_element_type=F32))


def _sb_weights(z, mask, suffix, c):
    log_keep = _neg_softplus(z)
    if mask is not None:
        log_keep = jnp.where(mask, log_keep, 0.0)
    incl = _suffix_sums(log_keep, suffix)
    w = jnp.exp(z + incl + c)
    if mask is not None:
        w = jnp.where(mask, w, 0.0)
    return w.astype(BF16), c + incl[:, :1]


def _prompt_attn_kernel(bias_ref, q_ref, kt_ref, v_ref, o_ref, *, blk, heads_per_step):
    hp = pl.program_id(1)
    i = pl.program_id(2)
    suffix = _suffix_matrix(blk)
    row = lax.broadcasted_iota(jnp.int32, (blk, blk), 0)
    col = lax.broadcasted_iota(jnp.int32, (blk, blk), 1)
    diag_mask = col < row
    outs = []
    for hh in range(heads_per_step):
        feats = slice(hh * HEAD_DIM, (hh + 1) * HEAD_DIM)
        bias = bias_ref[hp * heads_per_step + hh]
        q = q_ref[0, :, feats]

        def block(kb, carry, mask):
            c, acc = carry
            keys = pl.ds(pl.multiple_of(kb * blk, blk), blk)
            z = jnp.dot(q, kt_ref[0, feats, keys], preferred_element_type=F32) + bias
            w, c = _sb_weights(z, mask, suffix, c)
            acc = acc + jnp.dot(w, v_ref[0, keys, feats], preferred_element_type=F32)
            return c, acc

        carry = (jnp.zeros((blk, 1), F32), jnp.zeros((blk, HEAD_DIM), F32))
        carry = block(i, carry, diag_mask)
        carry = lax.fori_loop(0, i, lambda n, cr: block(i - 1 - n, cr, None), carry)
        outs.append(carry[1])
    o_ref[0] = jnp.concatenate(outs, axis=-1)


def _prompt_attention(q, kt, v, bias, blk=256, heads_per_step=2):
    b, t, d_attn = q.shape
    width = heads_per_step * HEAD_DIM
    return pl.pallas_call(
        functools.partial(_prompt_attn_kernel, blk=blk, heads_per_step=heads_per_step),
        grid=(b, d_attn // width, t // blk),
        in_specs=[
            pl.BlockSpec(memory_space=pltpu.SMEM),
            pl.BlockSpec((1, blk, width), lambda bi, hp, i: (bi, i, hp)),
            pl.BlockSpec((1, width, t), lambda bi, hp, i: (bi, hp, 0)),
            pl.BlockSpec((1, t, width), lambda bi, hp, i: (bi, 0, hp)),
        ],
        out_specs=pl.BlockSpec((1, blk, width), lambda bi, hp, i: (bi, i, hp)),
        out_shape=jax.ShapeDtypeStruct((b, t, d_attn), F32),
        compiler_params=_params("parallel", "parallel", "arbitrary"),
        name="prompt_attn",
    )(bias, q, kt, v)


def _sample_attn_kernel(pt_ref, bias_ref, q_ref, knt_ref, vnt_ref, *rest,
                        pages_per_step, n_heads, t_new):
    k_pages = rest[:pages_per_step]
    v_pages = rest[pages_per_step:2 * pages_per_step]
    o_ref = rest[2 * pages_per_step]
    qbd_ref, bias_scr, c_ref, acc_ref = rest[2 * pages_per_step + 1:]
    del pt_ref
    j = pl.program_id(1)
    rows_all = n_heads * t_new
    d_attn = n_heads * HEAD_DIM
    suffix = _suffix_matrix(PAGE_SIZE)

    def page(kt, vt, mask):
        z = jnp.dot(qbd_ref[...], kt, preferred_element_type=F32) + bias_scr[...]
        w, c = _sb_weights(z, mask, suffix, c_ref[...])
        c_ref[...] = c
        acc_ref[...] += lax.dot_general(w, vt, _NT, preferred_element_type=F32)

    @pl.when(j == 0)
    def _():
        r = lax.broadcasted_iota(jnp.int32, (rows_all, d_attn), 0)
        lane = lax.broadcasted_iota(jnp.int32, (rows_all, d_attn), 1)
        q_rep = jnp.concatenate([q_ref[0].astype(F32)] * n_heads, axis=0)
        qbd_ref[...] = jnp.where(r // t_new == lane // HEAD_DIM, q_rep, 0.0).astype(BF16)
        rb = lax.broadcasted_iota(jnp.int32, (rows_all, 1), 0) // t_new
        bias_col = jnp.zeros((rows_all, 1), F32)
        for h in range(n_heads):
            bias_col = jnp.where(rb == h, bias_ref[h], bias_col)
        bias_scr[...] = bias_col
        c_ref[...] = jnp.zeros_like(c_ref)
        acc_ref[...] = jnp.zeros_like(acc_ref)
        t_row = lax.broadcasted_iota(jnp.int32, (rows_all, PAGE_SIZE), 0) % t_new
        s_col = lax.broadcasted_iota(jnp.int32, (rows_all, PAGE_SIZE), 1)
        page(knt_ref[0], vnt_ref[0], s_col < t_row)

    for p in range(pages_per_step):
        page(k_pages[p][...].astype(BF16), v_pages[p][...].astype(BF16), None)

    @pl.when(j == pl.num_programs(1) - 1)
    def _():
        acc = acc_ref[...]
        lane = lax.broadcasted_iota(jnp.int32, (t_new, d_attn), 1) // HEAD_DIM
        out = jnp.zeros((t_new, d_attn), F32)
        for h in range(n_heads):
            out = jnp.where(lane == h, acc[h * t_new:(h + 1) * t_new, :], out)
        o_ref[0] = out


def _sample_attention(q, knt, vnt, bias, kt_cache, vt_cache, layer, page_table, pages_per_step=8):
    b, t_new, d_attn = q.shape
    n_heads = d_attn // HEAD_DIM
    n_pages = page_table.shape[1]
    steps = n_pages // pages_per_step
    rows_all = n_heads * t_new

    def page_spec(p):
        def index(bi, j, pt):
            return (layer, pt[bi * n_pages + (n_pages - 1 - (j * pages_per_step + p))], 0, 0)
        return pl.BlockSpec((None, None, d_attn, PAGE_SIZE), index)

    per_b = lambda rows, cols: pl.BlockSpec((1, rows, cols), lambda bi, j, pt: (bi, 0, 0))
    page_specs = [page_spec(p) for p in range(pages_per_step)]
    return pl.pallas_call(
        functools.partial(_sample_attn_kernel, pages_per_step=pages_per_step,
                          n_heads=n_heads, t_new=t_new),
        grid_spec=pltpu.PrefetchScalarGridSpec(
            num_scalar_prefetch=1,
            grid=(b, steps),
            in_specs=[pl.BlockSpec(memory_space=pltpu.SMEM), per_b(t_new, d_attn),
                      per_b(d_attn, PAGE_SIZE), per_b(d_attn, PAGE_SIZE)]
            + page_specs + page_specs,
            out_specs=per_b(t_new, d_attn),
            scratch_shapes=[
                pltpu.VMEM((rows_all, d_attn), BF16),
                pltpu.VMEM((rows_all, 1), F32),
                pltpu.VMEM((rows_all, 1), F32),
                pltpu.VMEM((rows_all, d_attn), F32),
            ],
        ),
        out_shape=jax.ShapeDtypeStruct((b, t_new, d_attn), F32),
        compiler_params=_params("parallel", "arbitrary"),
        name="sample_attn",
    )(page_table.reshape(-1), bias, q, knt, vnt,
      *([kt_cache] * pages_per_step), *([vt_cache] * pages_per_step))


def _silu(x):
    return x * (1.0 / (1.0 + jnp.exp(-x)))


def _mix_out_kernel(x_ref, a_ref, ga_ref, u_ref, uh_ref, pre_ref, gb_ref, wp_ref, ps_ref,
                    wo_ref, fg_ref, o_ref, seq_ref, *, tt, pos0, final):
    j = pl.program_id(1)
    seq_ref[0:HALO, :] = jnp.where(j == 0, pre_ref[0], uh_ref[0])
    seq_ref[HALO:, :] = u_ref[0]
    pos = pos0 + j * tt + lax.broadcasted_iota(jnp.int32, (tt, 1), 0)
    gb = gb_ref[0]
    parts = [a_ref[0] * _silu(ga_ref[0])]
    for g, w in enumerate(POOL_WINDOWS):
        lanes = slice(g * POOL_GROUP_DIM, (g + 1) * POOL_GROUP_DIM)
        win_sum = seq_ref[HALO:HALO + tt, lanes]
        for back in range(1, w):
            win_sum = win_sum + seq_ref[HALO - back:HALO - back + tt, lanes]
        count = jnp.minimum(pos + 1, w).astype(F32)
        diff = win_sum / count - seq_ref[HALO:HALO + tt, lanes]
        y = jnp.dot(diff.astype(BF16), wp_ref[g], preferred_element_type=F32) * ps_ref[:, lanes]
        parts.append(y * _silu(gb[:, lanes]))
    mix = jnp.concatenate(parts, axis=-1).astype(BF16)
    h = x_ref[0] + jnp.dot(mix, wo_ref[...], preferred_element_type=F32)
    if final:
        ms = jnp.mean(h * h, axis=-1, keepdims=True)
        h = h * lax.rsqrt(ms + RMS_EPS) * fg_ref[...]
    o_ref[0] = h


def _mix_out(x, a, gate_a, u, prefix, gate_b, w_pool, pool_scale, w_out, final_g, *, tt, pos0, final):
    b, t, d = x.shape
    d_attn = a.shape[-1]
    d_pool = u.shape[-1]
    tile = lambda width: pl.BlockSpec((1, tt, width), lambda bi, j: (bi, j, 0))
    const = lambda shape: pl.BlockSpec(shape, lambda bi, j: (0,) * len(shape))
    halo_blocks = tt // HALO if tt >= HALO else 1
    halo_spec = pl.BlockSpec(
        (1, HALO, d_pool), lambda bi, j: (bi, jnp.maximum(j * halo_blocks - 1, 0), 0))
    u_halo = u if t >= HALO else prefix
    return pl.pallas_call(
        functools.partial(_mix_out_kernel, tt=tt, pos0=pos0, final=final),
        grid=(b, t // tt),
        in_specs=[
            tile(d), tile(d_attn), tile(d_attn), tile(d_pool), halo_spec,
            pl.BlockSpec((1, HALO, d_pool), lambda bi, j: (bi, 0, 0)),
            tile(d_pool),
            const(w_pool.shape), const((1, d_pool)), const(w_out.shape), const((1, d)),
        ],
        out_specs=tile(d),
        out_shape=jax.ShapeDtypeStruct((b, t, d), F32),
        scratch_shapes=[pltpu.VMEM((HALO + tt, d_pool), F32)],
        compiler_params=_params("parallel", "arbitrary"),
        name="mix_out",
    )(x, a, gate_a, u, u_halo, prefix, gate_b, w_pool, pool_scale.reshape(1, d_pool), w_out,
      final_g.reshape(1, d))


def kernel(x_prompt, x_sample, cache_k, cache_v, state_pool, page_table, norm_g, w_in, sb_bias,
           w_pool, pool_scale, w_out, final_g):
    depth = norm_g.shape[0]
    bp, tp, d_model = x_prompt.shape
    bs, ts, _ = x_sample.shape
    n_heads = cache_k.shape[3]
    d_attn = n_heads * HEAD_DIM
    d_pool = state_pool.shape[-1]
    n_pool = cache_k.shape[1]
    past = page_table.shape[1] * PAGE_SIZE

    w_in_b = w_in.astype(BF16)
    wkvt_b = jnp.swapaxes(w_in_b[:, :, d_attn:3 * d_attn], 1, 2)
    w_pool_b = w_pool.astype(BF16)
    w_out_b = w_out.astype(BF16)
    kt_cache = cache_k.transpose(0, 1, 3, 4, 2).reshape(depth, n_pool, d_attn, PAGE_SIZE)
    vt_cache = cache_v.transpose(0, 1, 3, 4, 2).reshape(depth, n_pool, d_attn, PAGE_SIZE)

    def token_major(zt, b, t):
        return zt.reshape(b, n_heads, HEAD_DIM, t).transpose(0, 3, 1, 2)

    h = x_prompt
    zero_prefix = jnp.zeros((bp, HALO, d_pool), F32)
    k_prompt, v_prompt, pool_prompt = [], [], []
    for l in range(depth):
        q, kt, vt, ktb, vb, ga, u, gb = _norm_proj(
            h, norm_g[l], w_in_b[l], wkvt_b[l], d_attn, d_pool, tm=512, prompt=True)
        a = _prompt_attention(q, ktb, vb, sb_bias[l])
        h = _mix_out(h, a, ga, u, zero_prefix, gb, w_pool_b[l], pool_scale[l],
                     w_out_b[l], final_g, tt=512, pos0=0, final=(l == depth - 1))
        k_prompt.append(token_major(kt, bp, tp))
        v_prompt.append(token_major(vt, bp, tp))
        pool_prompt.append(u[:, tp - POOL_STATE_LEN:])
    y_prompt = h

    h = x_sample.reshape(1, bs * ts, d_model)
    k_sample, v_sample, pool_sample = [], [], []
    for l in range(depth):
        q, kf, vf, ktb, vtb, ga, u, gb = _norm_proj(
            h, norm_g[l], w_in_b[l], wkvt_b[l], d_attn, d_pool, tm=bs * ts, prompt=False)
        per_seq = lambda z: z.reshape(bs, ts, -1)

        def new_page(zt):
            zt = zt.reshape(d_attn, bs, ts).transpose(1, 0, 2)
            return jnp.pad(zt, ((0, 0), (0, 0), (0, PAGE_SIZE - ts)))

        a = _sample_attention(per_seq(q), new_page(ktb), new_page(vtb), sb_bias[l], kt_cache,
                              vt_cache, l, page_table)
        prefix = jnp.pad(state_pool[l], ((0, 0), (HALO - POOL_STATE_LEN, 0), (0, 0)))
        h = _mix_out(per_seq(h), a, per_seq(ga), per_seq(u), prefix, per_seq(gb), w_pool_b[l],
                     pool_scale[l], w_out_b[l], final_g, tt=ts, pos0=past,
                     final=(l == depth - 1)).reshape(1, bs * ts, d_model)
        k_sample.append(kf.reshape(bs, ts, n_heads, HEAD_DIM))
        v_sample.append(vf.reshape(bs, ts, n_heads, HEAD_DIM))
        pool_sample.append(jnp.concatenate([state_pool[l], per_seq(u)], axis=1)[:, -POOL_STATE_LEN:])
    y_sample = h.reshape(bs, ts, d_model)

    return (y_prompt, y_sample, jnp.stack(k_prompt), jnp.stack(v_prompt), jnp.stack(pool_prompt),
            jnp.stack(k_sample), jnp.stack(v_sample), jnp.stack(pool_sample))
```

```python
import functools
import math

import jax
import jax.numpy as jnp
from jax import lax
from jax.experimental import pallas as pl
from jax.experimental.pallas import tpu as pltpu

F32 = jnp.float32
BF16 = jnp.bfloat16

PAGE_SIZE = 128
HEAD_DIM = 64
POOL_WINDOWS = (2, 4, 8, 16)
POOL_GROUP_DIM = 128
POOL_STATE_LEN = max(POOL_WINDOWS) - 1
HALO = POOL_STATE_LEN + 1
RMS_EPS = 1e-6
SB_SCALE = 1.0 / math.sqrt(HEAD_DIM)
LOG2_E = math.log2(math.e)

VMEM_LIMIT_BYTES = 56 * 1024 * 1024

_NT = (((1,), (1,)), ((), ()))


def _params(*sem):
    return pltpu.CompilerParams(dimension_semantics=sem, vmem_limit_bytes=VMEM_LIMIT_BYTES)


def _norm_proj_kernel(x_ref, g_ref, w_ref, wkvt_ref, q_ref, k_ref, v_ref, kb_ref, vb_ref,
                      ga_ref, u_ref, gb_ref, *, d_attn, d_pool, prompt):
    x = x_ref[0]
    ms = jnp.mean(x * x, axis=-1, keepdims=True)
    h = (x * lax.rsqrt(ms + RMS_EPS) * g_ref[...]).astype(BF16)

    def proj(lo, width):
        return jnp.dot(h, w_ref[:, lo:lo + width], preferred_element_type=F32)

    q_ref[0] = (proj(0, d_attn) * -SB_SCALE).astype(BF16)
    kvt = lax.dot_general(wkvt_ref[...], h, _NT, preferred_element_type=F32)
    kb_ref[0] = kvt[:d_attn].astype(BF16)
    if prompt:
        k_ref[0] = kvt[:d_attn]
        v_ref[0] = kvt[d_attn:]
        vb_ref[0] = proj(2 * d_attn, d_attn).astype(BF16)
    else:
        k_ref[0] = proj(d_attn, d_attn)
        v_ref[0] = proj(2 * d_attn, d_attn)
        vb_ref[0] = kvt[d_attn:].astype(BF16)
    ga_ref[0] = proj(3 * d_attn, d_attn)
    u_ref[0] = proj(4 * d_attn, d_pool)
    gb_ref[0] = proj(4 * d_attn + d_pool, d_pool)


def _norm_proj(x, g, w_bf16, wkvt_bf16, d_attn, d_pool, tm, prompt):
    b, t, d = x.shape
    d_in = w_bf16.shape[1]
    tok = lambda width: pl.BlockSpec((1, tm, width), lambda bi, i: (bi, i, 0))
    feat = pl.BlockSpec((1, d_attn, tm), lambda bi, i: (bi, 0, i))
    tok_shape = lambda width, dt: jax.ShapeDtypeStruct((b, t, width), dt)
    feat_shape = lambda dt: jax.ShapeDtypeStruct((b, d_attn, t), dt)
    if prompt:
        kv_specs = [feat, feat, feat, tok(d_attn)]
        kv_shapes = [feat_shape(F32), feat_shape(F32), feat_shape(BF16), tok_shape(d_attn, BF16)]
    else:
        kv_specs = [tok(d_attn), tok(d_attn), feat, feat]
        kv_shapes = [tok_shape(d_attn, F32), tok_shape(d_attn, F32), feat_shape(BF16), feat_shape(BF16)]
    return pl.pallas_call(
        functools.partial(_norm_proj_kernel, d_attn=d_attn, d_pool=d_pool, prompt=prompt),
        grid=(b, t // tm),
        in_specs=[
            tok(d),
            pl.BlockSpec((1, d), lambda bi, i: (0, 0)),
            pl.BlockSpec((d, d_in), lambda bi, i: (0, 0)),
            pl.BlockSpec((2 * d_attn, d), lambda bi, i: (0, 0)),
        ],
        out_specs=[tok(d_attn)] + kv_specs + [tok(d_attn), tok(d_pool), tok(d_pool)],
        out_shape=[tok_shape(d_attn, BF16)] + kv_shapes
        + [tok_shape(d_attn, F32), tok_shape(d_pool, F32), tok_shape(d_pool, F32)],
        compiler_params=_params("parallel", "parallel"),
        name="norm_proj",
    )(x, g.reshape(1, d), w_bf16, wkvt_bf16)


def _log_keep(nz, mask):
    e = jnp.exp2(jnp.abs(nz) * -LOG2_E)
    log_keep = jnp.minimum(nz, 0.0) - jnp.log(1.0 + e)
    if mask is not None:
        log_keep = jnp.where(mask, log_keep, 0.0)
    return log_keep


def _suffix_matrix(n):
    j = lax.broadcasted_iota(jnp.int32, (2 * n, n), 0) % n
    s = lax.broadcasted_iota(jnp.int32, (2 * n, n), 1)
    return jnp.where(j >= s, 1.0, 0.0).astype(BF16)


def _suffix_sums(x, suffix):
    hi = x.astype(BF16)
    lo = (x - hi.astype(F32)).astype(BF16)
    return jnp.dot(jnp.concatenate([hi, lo], axis=1), suffix, preferred_element_type=F32)


def _sb_weights(nz, incl, c, mask):
    w = jnp.exp(incl + c - nz)
    if mask is not None:
        w = jnp.where(mask, w, 0.0)
    return w.astype(BF16)


def _prompt_attn_kernel(bias_ref, q_ref, kt_ref, v_ref, o_ref, c_ref, acc_ref, *, blk,
                        heads_per_step):
    hp = pl.program_id(1)
    i = pl.program_id(2)
    suffix = _suffix_matrix(blk)
    row = lax.broadcasted_iota(jnp.int32, (blk, blk), 0)
    col = lax.broadcasted_iota(jnp.int32, (blk, blk), 1)
    diag_mask = col < row
    feats = [slice(hh * HEAD_DIM, (hh + 1) * HEAD_DIM) for hh in range(heads_per_step)]
    bias = [bias_ref[hp * heads_per_step + hh] for hh in range(heads_per_step)]
    qn = [q_ref[0, :, f] for f in feats]
    heads = range(heads_per_step)

    def block(kb, mask):
        keys = pl.ds(pl.multiple_of(kb * blk, blk), blk)
        nz = [jnp.dot(qn[h], kt_ref[0, feats[h], keys], preferred_element_type=F32) - bias[h]
              for h in heads]
        log_keep = [_log_keep(nz[h], mask) for h in heads]
        incl = [_suffix_sums(log_keep[h], suffix) for h in heads]
        for h in heads:
            c = c_ref[h]
            w = _sb_weights(nz[h], incl[h], jnp.concatenate([c] * (blk // 128), axis=1), mask)
            acc_ref[h] += jnp.dot(w, v_ref[0, keys, feats[h]], preferred_element_type=F32)
            c_ref[h] = c + jnp.broadcast_to(incl[h][:, :1], c.shape)

    c_ref[...] = jnp.zeros_like(c_ref)
    acc_ref[...] = jnp.zeros_like(acc_ref)
    block(i, diag_mask)

    def body(n, _):
        block(i - 1 - n, None)
        return 0

    lax.fori_loop(0, i, body, 0)
    o_ref[0] = jnp.concatenate([acc_ref[h] for h in heads], axis=-1)


def _prompt_attention(q, kt, v, bias, blk=256, heads_per_step=4):
    b, t, d_attn = q.shape
    width = heads_per_step * HEAD_DIM
    return pl.pallas_call(
        functools.partial(_prompt_attn_kernel, blk=blk, heads_per_step=heads_per_step),
        grid=(b, d_attn // width, t // blk),
        in_specs=[
            pl.BlockSpec(memory_space=pltpu.SMEM),
            pl.BlockSpec((1, blk, width), lambda bi, hp, i: (bi, i, hp)),
            pl.BlockSpec((1, width, t), lambda bi, hp, i: (bi, hp, 0)),
            pl.BlockSpec((1, t, width), lambda bi, hp, i: (bi, 0, hp)),
        ],
        out_specs=pl.BlockSpec((1, blk, width), lambda bi, hp, i: (bi, i, hp)),
        out_shape=jax.ShapeDtypeStruct((b, t, d_attn), F32),
        scratch_shapes=[pltpu.VMEM((heads_per_step, blk, 128), F32),
                        pltpu.VMEM((heads_per_step, blk, HEAD_DIM), F32)],
        compiler_params=_params("parallel", "parallel", "arbitrary"),
        name="prompt_attn",
    )(bias, q, kt, v)


def _sample_attn_kernel(pt_ref, bias_ref, q_ref, knt_ref, vnt_ref, *rest,
                        pages_per_step, n_heads, t_new):
    k_pages = rest[:pages_per_step]
    v_pages = rest[pages_per_step:2 * pages_per_step]
    o_ref = rest[2 * pages_per_step]
    qbd_ref, bias_scr, c_ref, acc_ref = rest[2 * pages_per_step + 1:]
    del pt_ref
    j = pl.program_id(1)
    rows_all = n_heads * t_new
    d_attn = n_heads * HEAD_DIM
    suffix = _suffix_matrix(PAGE_SIZE)

    def visit(kts, vts, mask):
        cols = [slice(p * PAGE_SIZE, (p + 1) * PAGE_SIZE) for p in range(len(kts))]
        nz = jnp.dot(qbd_ref[...], jnp.concatenate(kts, axis=1),
                     preferred_element_type=F32) - bias_scr[...]
        log_keep = _log_keep(nz, mask)
        incl = _suffix_sums(jnp.concatenate([log_keep[:, s] for s in cols], axis=0), suffix)
        c = c_ref[...]
        ws = []
        for p, s in enumerate(cols):
            incl_p = incl[p * rows_all:(p + 1) * rows_all]
            ws.append(_sb_weights(nz[:, s], incl_p, c, mask))
            c = c + incl_p[:, :1]
        c_ref[...] = c
        acc_ref[...] += lax.dot_general(jnp.concatenate(ws, axis=1), jnp.concatenate(vts, axis=1),
                                        _NT, preferred_element_type=F32)

    @pl.when(j == 0)
    def _():
        r = lax.broadcasted_iota(jnp.int32, (rows_all, d_attn), 0)
        lane = lax.broadcasted_iota(jnp.int32, (rows_all, d_attn), 1)
        q_rep = jnp.concatenate([q_ref[0].astype(F32)] * n_heads, axis=0)
        qbd_ref[...] = jnp.where(r // t_new == lane // HEAD_DIM, q_rep, 0.0).astype(BF16)
        rb = lax.broadcasted_iota(jnp.int32, (rows_all, 1), 0) // t_new
        bias_col = jnp.zeros((rows_all, 1), F32)
        for h in range(n_heads):
            bias_col = jnp.where(rb == h, bias_ref[h], bias_col)
        bias_scr[...] = bias_col
        c_ref[...] = jnp.zeros_like(c_ref)
        acc_ref[...] = jnp.zeros_like(acc_ref)
        t_row = lax.broadcasted_iota(jnp.int32, (rows_all, PAGE_SIZE), 0) % t_new
        s_col = lax.broadcasted_iota(jnp.int32, (rows_all, PAGE_SIZE), 1)
        visit([knt_ref[0]], [vnt_ref[0]], s_col < t_row)

    visit([r[...].astype(BF16) for r in k_pages], [r[...].astype(BF16) for r in v_pages], None)

    @pl.when(j == pl.num_programs(1) - 1)
    def _():
        acc = acc_ref[...]
        lane = lax.broadcasted_iota(jnp.int32, (t_new, d_attn), 1) // HEAD_DIM
        out = jnp.zeros((t_new, d_attn), F32)
        for h in range(n_heads):
            out = jnp.where(lane == h, acc[h * t_new:(h + 1) * t_new, :], out)
        o_ref[0] = out


def _sample_attention(q, knt, vnt, bias, kt_cache, vt_cache, layer, page_table, pages_per_step=16):
    b, t_new, d_attn = q.shape
    n_heads = d_attn // HEAD_DIM
    n_pages = page_table.shape[1]
    steps = n_pages // pages_per_step
    rows_all = n_heads * t_new

    def page_spec(p):
        def index(bi, j, pt):
            return (layer, pt[bi * n_pages + (n_pages - 1 - (j * pages_per_step + p))], 0, 0)
        return pl.BlockSpec((None, None, d_attn, PAGE_SIZE), index)

    per_b = lambda rows, cols: pl.BlockSpec((1, rows, cols), lambda bi, j, pt: (bi, 0, 0))
    page_specs = [page_spec(p) for p in range(pages_per_step)]
    return pl.pallas_call(
        functools.partial(_sample_attn_kernel, pages_per_step=pages_per_step,
                          n_heads=n_heads, t_new=t_new),
        grid_spec=pltpu.PrefetchScalarGridSpec(
            num_scalar_prefetch=1,
            grid=(b, steps),
            in_specs=[pl.BlockSpec(memory_space=pltpu.SMEM), per_b(t_new, d_attn),
                      per_b(d_attn, PAGE_SIZE), per_b(d_attn, PAGE_SIZE)]
            + page_specs + page_specs,
            out_specs=per_b(t_new, d_attn),
            scratch_shapes=[
                pltpu.VMEM((rows_all, d_attn), BF16),
                pltpu.VMEM((rows_all, 1), F32),
                pltpu.VMEM((rows_all, 1), F32),
                pltpu.VMEM((rows_all, d_attn), F32),
            ],
        ),
        out_shape=jax.ShapeDtypeStruct((b, t_new, d_attn), F32),
        compiler_params=_params("parallel", "arbitrary"),
        name="sample_attn",
    )(page_table.reshape(-1), bias, q, knt, vnt,
      *([kt_cache] * pages_per_step), *([vt_cache] * pages_per_step))


def _silu(x):
    return x * (1.0 / (1.0 + jnp.exp(-x)))


def _mix_out_kernel(x_ref, a_ref, ga_ref, u_ref, uh_ref, pre_ref, gb_ref, wp_ref, ps_ref,
                    wo_ref, fg_ref, o_ref, seq_ref, *, tt, pos0, final):
    j = pl.program_id(1)
    seq_ref[0:HALO, :] = jnp.where(j == 0, pre_ref[0], uh_ref[0])
    seq_ref[HALO:, :] = u_ref[0]
    pos = pos0 + j * tt + lax.broadcasted_iota(jnp.int32, (tt, 1), 0)
    gb = gb_ref[0]
    parts = [a_ref[0] * _silu(ga_ref[0])]
    for g, w in enumerate(POOL_WINDOWS):
        lanes = slice(g * POOL_GROUP_DIM, (g + 1) * POOL_GROUP_DIM)
        win_sum = seq_ref[HALO:HALO + tt, lanes]
        for back in range(1, w):
            win_sum = win_sum + seq_ref[HALO - back:HALO - back + tt, lanes]
        count = jnp.minimum(pos + 1, w).astype(F32)
        diff = win_sum / count - seq_ref[HALO:HALO + tt, lanes]
        y = jnp.dot(diff.astype(BF16), wp_ref[g], preferred_element_type=F32) * ps_ref[:, lanes]
        parts.append(y * _silu(gb[:, lanes]))
    mix = jnp.concatenate(parts, axis=-1).astype(BF16)
    h = x_ref[0] + jnp.dot(mix, wo_ref[...], preferred_element_type=F32)
    if final:
        ms = jnp.mean(h * h, axis=-1, keepdims=True)
        h = h * lax.rsqrt(ms + RMS_EPS) * fg_ref[...]
    o_ref[0] = h


def _mix_out(x, a, gate_a, u, prefix, gate_b, w_pool, pool_scale, w_out, final_g, *, tt, pos0, final):
    b, t, d = x.shape
    d_attn = a.shape[-1]
    d_pool = u.shape[-1]
    tile = lambda width: pl.BlockSpec((1, tt, width), lambda bi, j: (bi, j, 0))
    const = lambda shape: pl.BlockSpec(shape, lambda bi, j: (0,) * len(shape))
    halo_blocks = tt // HALO if tt >= HALO else 1
    halo_spec = pl.BlockSpec(
        (1, HALO, d_pool), lambda bi, j: (bi, jnp.maximum(j * halo_blocks - 1, 0), 0))
    u_halo = u if t >= HALO else prefix
    return pl.pallas_call(
        functools.partial(_mix_out_kernel, tt=tt, pos0=pos0, final=final),
        grid=(b, t // tt),
        in_specs=[
            tile(d), tile(d_attn), tile(d_attn), tile(d_pool), halo_spec,
            pl.BlockSpec((1, HALO, d_pool), lambda bi, j: (bi, 0, 0)),
            tile(d_pool),
            const(w_pool.shape), const((1, d_pool)), const(w_out.shape), const((1, d)),
        ],
        out_specs=tile(d),
        out_shape=jax.ShapeDtypeStruct((b, t, d), F32),
        scratch_shapes=[pltpu.VMEM((HALO + tt, d_pool), F32)],
        compiler_params=_params("parallel", "arbitrary"),
        name="mix_out",
    )(x, a, gate_a, u, u_halo, prefix, gate_b, w_pool, pool_scale.reshape(1, d_pool), w_out,
      final_g.reshape(1, d))


def kernel(x_prompt, x_sample, cache_k, cache_v, state_pool, page_table, norm_g, w_in, sb_bias,
           w_pool, pool_scale, w_out, final_g):
    depth = norm_g.shape[0]
    bp, tp, d_model = x_prompt.shape
    bs, ts, _ = x_sample.shape
    n_heads = cache_k.shape[3]
    d_attn = n_heads * HEAD_DIM
    d_pool = state_pool.shape[-1]
    n_pool = cache_k.shape[1]
    past = page_table.shape[1] * PAGE_SIZE

    w_in_b = w_in.astype(BF16)
    wkvt_b = jnp.swapaxes(w_in_b[:, :, d_attn:3 * d_attn], 1, 2)
    w_pool_b = w_pool.astype(BF16)
    w_out_b = w_out.astype(BF16)
    kt_cache = cache_k.transpose(0, 1, 3, 4, 2).reshape(depth, n_pool, d_attn, PAGE_SIZE)
    vt_cache = cache_v.transpose(0, 1, 3, 4, 2).reshape(depth, n_pool, d_attn, PAGE_SIZE)

    def token_major(zt, b, t):
        return zt.reshape(b, n_heads, HEAD_DIM, t).transpose(0, 3, 1, 2)

    h = x_prompt
    zero_prefix = jnp.zeros((bp, HALO, d_pool), F32)
    k_prompt, v_prompt, pool_prompt = [], [], []
    for l in range(depth):
        q, kt, vt, ktb, vb, ga, u, gb = _norm_proj(
            h, norm_g[l], w_in_b[l], wkvt_b[l], d_attn, d_pool, tm=512, prompt=True)
        a = _prompt_attention(q, ktb, vb, sb_bias[l])
        h = _mix_out(h, a, ga, u, zero_prefix, gb, w_pool_b[l], pool_scale[l],
                     w_out_b[l], final_g, tt=512, pos0=0, final=(l == depth - 1))
        k_prompt.append(token_major(kt, bp, tp))
        v_prompt.append(token_major(vt, bp, tp))
        pool_prompt.append(u[:, tp - POOL_STATE_LEN:])
    y_prompt = h

    h = x_sample.reshape(1, bs * ts, d_model)
    k_sample, v_sample, pool_sample = [], [], []
    for l in range(depth):
        q, kf, vf, ktb, vtb, ga, u, gb = _norm_proj(
            h, norm_g[l], w_in_b[l], wkvt_b[l], d_attn, d_pool, tm=bs * ts, prompt=False)
        per_seq = lambda z: z.reshape(bs, ts, -1)

        def new_page(zt):
            zt = zt.reshape(d_attn, bs, ts).transpose(1, 0, 2)
            return jnp.pad(zt, ((0, 0), (0, 0), (0, PAGE_SIZE - ts)))

        a = _sample_attention(per_seq(q), new_page(ktb), new_page(vtb), sb_bias[l], kt_cache,
                              vt_cache, l, page_table)
        prefix = jnp.pad(state_pool[l], ((0, 0), (HALO - POOL_STATE_LEN, 0), (0, 0)))
        h = _mix_out(per_seq(h), a, per_seq(ga), per_seq(u), prefix, per_seq(gb), w_pool_b[l],
                     pool_scale[l], w_out_b[l], final_g, tt=ts, pos0=past,
                     final=(l == depth - 1)).reshape(1, bs * ts, d_model)
        k_sample.append(kf.reshape(bs, ts, n_heads, HEAD_DIM))
        v_sample.append(vf.reshape(bs, ts, n_heads, HEAD_DIM))
        pool_sample.append(jnp.concatenate([state_pool[l], per_seq(u)], axis=1)[:, -POOL_STATE_LEN:])
    y_sample = h.reshape(bs, ts, d_model)

    return (y_prompt, y_sample, jnp.stack(k_prompt), jnp.stack(v_prompt), jnp.stack(pool_prompt),
            jnp.stack(k_sample), jnp.stack(v_sample), jnp.stack(pool_sample))
```

```python
import functools
import math

import jax
import jax.numpy as jnp
from jax import lax
from jax.experimental import pallas as pl
from jax.experimental.pallas import tpu as pltpu

F32 = jnp.float32
BF16 = jnp.bfloat16

PAGE_SIZE = 128
HEAD_DIM = 64
POOL_WINDOWS = (2, 4, 8, 16)
POOL_GROUP_DIM = 128
POOL_STATE_LEN = max(POOL_WINDOWS) - 1
HALO = POOL_STATE_LEN + 1
RMS_EPS = 1e-6
SB_SCALE = 1.0 / math.sqrt(HEAD_DIM)
LOG2_E = math.log2(math.e)

VMEM_LIMIT_BYTES = 56 * 1024 * 1024

_NT = (((1,), (1,)), ((), ()))


def _params(*sem):
    return pltpu.CompilerParams(dimension_semantics=sem, vmem_limit_bytes=VMEM_LIMIT_BYTES)


def _norm_proj_kernel(x_ref, g_ref, w_ref, wkvt_ref, *rest, d_attn, d_pool, prompt):
    q_ref, k_ref, v_ref, kb_ref, vb_ref, ga_ref, u_ref, gb_ref = rest[-8:]
    x = x_ref[0]
    ms = jnp.mean(x * x, axis=-1, keepdims=True)
    h = (x * lax.rsqrt(ms + RMS_EPS) * g_ref[...]).astype(BF16)

    def proj(lo, width):
        return jnp.dot(h, w_ref[:, lo:lo + width], preferred_element_type=F32)

    q_ref[0] = (proj(0, d_attn) * -SB_SCALE).astype(BF16)
    kvt = lax.dot_general(wkvt_ref[...], h, _NT, preferred_element_type=F32)
    kb_ref[0] = kvt[:d_attn].astype(BF16)
    if prompt:
        k_ref[0] = kvt[:d_attn]
        v_ref[0] = kvt[d_attn:]
        vb_ref[0] = proj(2 * d_attn, d_attn).astype(BF16)
    else:
        k_ref[0] = proj(d_attn, d_attn)
        v_ref[0] = proj(2 * d_attn, d_attn)
        vb_ref[0] = kvt[d_attn:].astype(BF16)
    ga_ref[0] = proj(3 * d_attn, d_attn)
    u_ref[0] = proj(4 * d_attn, d_pool)
    gb_ref[0] = proj(4 * d_attn + d_pool, d_pool)


def _norm_proj(x, g, w_bf16, wkvt_bf16, d_attn, d_pool, tm, prompt, layer=0, depth=1, kv_prev=None):
    b, t, d = x.shape
    d_in = w_bf16.shape[1]
    tok = lambda width: pl.BlockSpec((1, tm, width), lambda bi, i: (bi, i, 0))
    feat = pl.BlockSpec((1, d_attn, tm), lambda bi, i: (bi, 0, i))
    tok_shape = lambda width, dt: jax.ShapeDtypeStruct((b, t, width), dt)
    feat_shape = lambda dt: jax.ShapeDtypeStruct((b, d_attn, t), dt)
    if prompt:
        layered = pl.BlockSpec((None, 1, d_attn, tm), lambda bi, i: (layer, bi, 0, i))
        layered_shape = jax.ShapeDtypeStruct((depth, b, d_attn, t), F32)
        kv_specs = [layered, layered, feat, tok(d_attn)]
        kv_shapes = [layered_shape, layered_shape, feat_shape(BF16), tok_shape(d_attn, BF16)]
    else:
        kv_specs = [tok(d_attn), tok(d_attn), feat, feat]
        kv_shapes = [tok_shape(d_attn, F32), tok_shape(d_attn, F32), feat_shape(BF16), feat_shape(BF16)]
    prev = list(kv_prev) if kv_prev is not None else []
    n_in = 4
    return pl.pallas_call(
        functools.partial(_norm_proj_kernel, d_attn=d_attn, d_pool=d_pool, prompt=prompt),
        grid=(b, t // tm),
        in_specs=[
            tok(d),
            pl.BlockSpec((1, d), lambda bi, i: (0, 0)),
            pl.BlockSpec((d, d_in), lambda bi, i: (0, 0)),
            pl.BlockSpec((2 * d_attn, d), lambda bi, i: (0, 0)),
        ] + [pl.BlockSpec(memory_space=pl.ANY)] * len(prev),
        out_specs=[tok(d_attn)] + kv_specs + [tok(d_attn), tok(d_pool), tok(d_pool)],
        out_shape=[tok_shape(d_attn, BF16)] + kv_shapes
        + [tok_shape(d_attn, F32), tok_shape(d_pool, F32), tok_shape(d_pool, F32)],
        input_output_aliases={n_in + n: 1 + n for n in range(len(prev))},
        compiler_params=_params("parallel", "parallel"),
        name="norm_proj",
    )(x, g.reshape(1, d), w_bf16, wkvt_bf16, *prev)


def _log_keep(nz, mask):
    e = jnp.exp2(jnp.abs(nz) * -LOG2_E)
    log_keep = jnp.minimum(nz, 0.0) - jnp.log(1.0 + e)
    if mask is not None:
        log_keep = jnp.where(mask, log_keep, 0.0)
    return log_keep


def _suffix_matrix(n):
    j = lax.broadcasted_iota(jnp.int32, (2 * n, n), 0) % n
    s = lax.broadcasted_iota(jnp.int32, (2 * n, n), 1)
    return jnp.where(j >= s, 1.0, 0.0).astype(BF16)


def _suffix_sums(x, suffix):
    hi = x.astype(BF16)
    lo = (x - hi.astype(F32)).astype(BF16)
    return jnp.dot(jnp.concatenate([hi, lo], axis=1), suffix, preferred_element_type=F32)


def _sb_weights(nz, incl, c, mask):
    w = jnp.exp(incl + c - nz)
    if mask is not None:
        w = jnp.where(mask, w, 0.0)
    return w.astype(BF16)


SAMPLE_PAGES_PER_VISIT = 16


def _prompt_part(bias_ref, q_ref, kt_ref, v_ref, o_ref, c_ref, acc_ref, *, hp, i, blk,
                 heads_per_step, between):
    suffix = _suffix_matrix(blk)
    row = lax.broadcasted_iota(jnp.int32, (blk, blk), 0)
    col = lax.broadcasted_iota(jnp.int32, (blk, blk), 1)
    diag_mask = col < row
    feats = [slice(hh * HEAD_DIM, (hh + 1) * HEAD_DIM) for hh in range(heads_per_step)]
    bias = [bias_ref[hp * heads_per_step + hh] for hh in range(heads_per_step)]
    qn = [q_ref[0, :, f] for f in feats]
    heads = range(heads_per_step)

    def block(kb, mask):
        keys = pl.ds(pl.multiple_of(kb * blk, blk), blk)
        nz = [jnp.dot(qn[h], kt_ref[0, feats[h], keys], preferred_element_type=F32) - bias[h]
              for h in heads]
        log_keep = [_log_keep(nz[h], mask) for h in heads]
        incl = [_suffix_sums(log_keep[h], suffix) for h in heads]
        for h in heads:
            c = c_ref[h]
            w = _sb_weights(nz[h], incl[h], jnp.concatenate([c] * (blk // 128), axis=1), mask)
            acc_ref[h] += jnp.dot(w, v_ref[0, keys, feats[h]], preferred_element_type=F32)
            c_ref[h] = c + jnp.broadcast_to(incl[h][:, :1], c.shape)

    c_ref[...] = jnp.zeros_like(c_ref)
    acc_ref[...] = jnp.zeros_like(acc_ref)
    block(i, diag_mask)
    between()

    def body(n, _):
        block(i - 1 - n, None)
        return 0

    lax.fori_loop(0, i, body, 0)
    o_ref[0] = jnp.concatenate([acc_ref[h] for h in heads], axis=-1)


def _sample_init(bias_ref, q_ref, qbd_ref, bias_scr, c_ref, acc_ref, *, n_heads, t_new):
    rows_all, d_attn = qbd_ref.shape
    r = lax.broadcasted_iota(jnp.int32, (rows_all, d_attn), 0)
    lane = lax.broadcasted_iota(jnp.int32, (rows_all, d_attn), 1)
    q_rep = jnp.concatenate([q_ref[0].astype(F32)] * n_heads, axis=0)
    qbd_ref[...] = jnp.where(r // t_new == lane // HEAD_DIM, q_rep, 0.0).astype(BF16)
    rb = lax.broadcasted_iota(jnp.int32, (rows_all, 1), 0) // t_new
    bias_col = jnp.zeros((rows_all, 1), F32)
    for h in range(n_heads):
        bias_col = jnp.where(rb == h, bias_ref[h], bias_col)
    bias_scr[...] = bias_col
    c_ref[...] = jnp.zeros_like(c_ref)
    acc_ref[...] = jnp.zeros_like(acc_ref)


def _sample_visit(kts, vts, mask, qbd_ref, bias_scr, c_ref, acc_ref):
    rows_all = qbd_ref.shape[0]
    suffix = _suffix_matrix(PAGE_SIZE)
    cols = [slice(p * PAGE_SIZE, (p + 1) * PAGE_SIZE) for p in range(len(kts))]
    nz = jnp.dot(qbd_ref[...], jnp.concatenate(kts, axis=1),
                 preferred_element_type=F32) - bias_scr[...]
    log_keep = _log_keep(nz, mask)
    incl = _suffix_sums(jnp.concatenate([log_keep[:, s] for s in cols], axis=0), suffix)
    c = c_ref[...]
    ws = []
    for p, s in enumerate(cols):
        incl_p = incl[p * rows_all:(p + 1) * rows_all]
        ws.append(_sb_weights(nz[:, s], incl_p, c, mask))
        c = c + incl_p[:, :1]
    c_ref[...] = c
    acc_ref[...] += lax.dot_general(jnp.concatenate(ws, axis=1), jnp.concatenate(vts, axis=1),
                                    _NT, preferred_element_type=F32)


def _attn_kernel(pt_ref, bias_ref, q_ref, kt_ref, v_ref, sq_ref, knt_ref, vnt_ref, *rest,
                 blk, heads_per_step, pages_per_step, chunks_per_seq, n_heads, t_new):
    k_pages = rest[:pages_per_step]
    v_pages = rest[pages_per_step:2 * pages_per_step]
    o_ref, so_ref = rest[2 * pages_per_step:2 * pages_per_step + 2]
    c_ref, acc_ref, qbd_ref, bias_scr, sc_ref, sacc_ref = rest[2 * pages_per_step + 2:]
    del pt_ref
    hp = pl.program_id(1)
    i = pl.program_id(2)
    step = (pl.program_id(0) * pl.num_programs(1) + hp) * pl.num_programs(2) + i
    chunk = step % chunks_per_seq
    d_attn = n_heads * HEAD_DIM
    sample_state = (qbd_ref, bias_scr, sc_ref, sacc_ref)

    @pl.when(chunk == 0)
    def _():
        _sample_init(bias_ref, sq_ref, *sample_state, n_heads=n_heads, t_new=t_new)
        rows_all = n_heads * t_new
        t_row = lax.broadcasted_iota(jnp.int32, (rows_all, PAGE_SIZE), 0) % t_new
        s_col = lax.broadcasted_iota(jnp.int32, (rows_all, PAGE_SIZE), 1)
        _sample_visit([knt_ref[0]], [vnt_ref[0]], s_col < t_row, *sample_state)

    def past_pages():
        for lo in range(0, pages_per_step, SAMPLE_PAGES_PER_VISIT):
            sel = slice(lo, lo + SAMPLE_PAGES_PER_VISIT)
            _sample_visit([r[...].astype(BF16) for r in k_pages[sel]],
                          [r[...].astype(BF16) for r in v_pages[sel]], None, *sample_state)

    _prompt_part(bias_ref, q_ref, kt_ref, v_ref, o_ref, c_ref, acc_ref, hp=hp, i=i, blk=blk,
                 heads_per_step=heads_per_step, between=past_pages)

    @pl.when(chunk == chunks_per_seq - 1)
    def _():
        acc = sacc_ref[...]
        lane = lax.broadcasted_iota(jnp.int32, (t_new, d_attn), 1) // HEAD_DIM
        out = jnp.zeros((t_new, d_attn), F32)
        for h in range(n_heads):
            out = jnp.where(lane == h, acc[h * t_new:(h + 1) * t_new, :], out)
        so_ref[0] = out


def _attention(q, kt, v, sq, knt, vnt, bias, kt_cache, vt_cache, layer, page_table, blk=256,
               heads_per_step=4):
    b, t, d_attn = q.shape
    bs, t_new, _ = sq.shape
    n_heads = d_attn // HEAD_DIM
    n_pages = page_table.shape[1]
    width = heads_per_step * HEAD_DIM
    grid = (b, d_attn // width, t // blk)
    n_steps = grid[0] * grid[1] * grid[2]
    chunks_per_seq, rem = divmod(n_steps, bs)
    pages_per_step, rem2 = divmod(n_pages, chunks_per_seq)
    assert rem == 0 and rem2 == 0 and pages_per_step % SAMPLE_PAGES_PER_VISIT == 0
    rows_all = n_heads * t_new

    def step_of(bi, hp, i):
        return (bi * grid[1] + hp) * grid[2] + i

    def page_spec(p):
        def index(bi, hp, i, pt):
            s = step_of(bi, hp, i)
            page = n_pages - 1 - ((s % chunks_per_seq) * pages_per_step + p)
            return (layer, pt[(s // chunks_per_seq) * n_pages + page], 0, 0)
        return pl.BlockSpec((None, None, d_attn, PAGE_SIZE), index)

    per_seq = lambda rows, cols: pl.BlockSpec(
        (1, rows, cols), lambda bi, hp, i, pt: (step_of(bi, hp, i) // chunks_per_seq, 0, 0))
    page_specs = [page_spec(p) for p in range(pages_per_step)]
    return pl.pallas_call(
        functools.partial(_attn_kernel, blk=blk, heads_per_step=heads_per_step,
                          pages_per_step=pages_per_step, chunks_per_seq=chunks_per_seq,
                          n_heads=n_heads, t_new=t_new),
        grid_spec=pltpu.PrefetchScalarGridSpec(
            num_scalar_prefetch=1,
            grid=grid,
            in_specs=[
                pl.BlockSpec(memory_space=pltpu.SMEM),
                pl.BlockSpec((1, blk, width), lambda bi, hp, i, pt: (bi, i, hp)),
                pl.BlockSpec((1, width, t), lambda bi, hp, i, pt: (bi, hp, 0)),
                pl.BlockSpec((1, t, width), lambda bi, hp, i, pt: (bi, 0, hp)),
                per_seq(t_new, d_attn), per_seq(d_attn, PAGE_SIZE), per_seq(d_attn, PAGE_SIZE),
            ] + page_specs + page_specs,
            out_specs=[pl.BlockSpec((1, blk, width), lambda bi, hp, i, pt: (bi, i, hp)),
                       per_seq(t_new, d_attn)],
            scratch_shapes=[
                pltpu.VMEM((heads_per_step, blk, 128), F32),
                pltpu.VMEM((heads_per_step, blk, HEAD_DIM), F32),
                pltpu.VMEM((rows_all, d_attn), BF16),
                pltpu.VMEM((rows_all, 1), F32),
                pltpu.VMEM((rows_all, 1), F32),
                pltpu.VMEM((rows_all, d_attn), F32),
            ],
        ),
        out_shape=[jax.ShapeDtypeStruct((b, t, d_attn), F32),
                   jax.ShapeDtypeStruct((bs, t_new, d_attn), F32)],
        compiler_params=_params("arbitrary", "arbitrary", "arbitrary"),
        name="sb_attn",
    )(page_table.reshape(-1), bias, q, kt, v, sq, knt, vnt,
      *([kt_cache] * pages_per_step), *([vt_cache] * pages_per_step))


def _silu(x):
    return x * (1.0 / (1.0 + jnp.exp(-x)))


def _mix_out_kernel(x_ref, a_ref, ga_ref, u_ref, uh_ref, pre_ref, gb_ref, wp_ref, ps_ref,
                    wo_ref, fg_ref, o_ref, seq_ref, *, tt, pos0, final):
    j = pl.program_id(1)
    seq_ref[0:HALO, :] = jnp.where(j == 0, pre_ref[0], uh_ref[0])
    seq_ref[HALO:, :] = u_ref[0]
    pos = pos0 + j * tt + lax.broadcasted_iota(jnp.int32, (tt, 1), 0)
    gb = gb_ref[0]
    parts = [a_ref[0] * _silu(ga_ref[0])]
    for g, w in enumerate(POOL_WINDOWS):
        lanes = slice(g * POOL_GROUP_DIM, (g + 1) * POOL_GROUP_DIM)
        win_sum = seq_ref[HALO:HALO + tt, lanes]
        for back in range(1, w):
            win_sum = win_sum + seq_ref[HALO - back:HALO - back + tt, lanes]
        count = jnp.minimum(pos + 1, w).astype(F32)
        diff = win_sum / count - seq_ref[HALO:HALO + tt, lanes]
        y = jnp.dot(diff.astype(BF16), wp_ref[g], preferred_element_type=F32) * ps_ref[:, lanes]
        parts.append(y * _silu(gb[:, lanes]))
    mix = jnp.concatenate(parts, axis=-1).astype(BF16)
    h = x_ref[0] + jnp.dot(mix, wo_ref[...], preferred_element_type=F32)
    if final:
        ms = jnp.mean(h * h, axis=-1, keepdims=True)
        h = h * lax.rsqrt(ms + RMS_EPS) * fg_ref[...]
    o_ref[0] = h


def _mix_out(x, a, gate_a, u, prefix, gate_b, w_pool, pool_scale, w_out, final_g, *, tt, pos0, final):
    b, t, d = x.shape
    d_attn = a.shape[-1]
    d_pool = u.shape[-1]
    tile = lambda width: pl.BlockSpec((1, tt, width), lambda bi, j: (bi, j, 0))
    const = lambda shape: pl.BlockSpec(shape, lambda bi, j: (0,) * len(shape))
    halo_blocks = tt // HALO if tt >= HALO else 1
    halo_spec = pl.BlockSpec(
        (1, HALO, d_pool), lambda bi, j: (bi, jnp.maximum(j * halo_blocks - 1, 0), 0))
    u_halo = u if t >= HALO else prefix
    return pl.pallas_call(
        functools.partial(_mix_out_kernel, tt=tt, pos0=pos0, final=final),
        grid=(b, t // tt),
        in_specs=[
            tile(d), tile(d_attn), tile(d_attn), tile(d_pool), halo_spec,
            pl.BlockSpec((1, HALO, d_pool), lambda bi, j: (bi, 0, 0)),
            tile(d_pool),
            const(w_pool.shape), const((1, d_pool)), const(w_out.shape), const((1, d)),
        ],
        out_specs=tile(d),
        out_shape=jax.ShapeDtypeStruct((b, t, d), F32),
        scratch_shapes=[pltpu.VMEM((HALO + tt, d_pool), F32)],
        compiler_params=_params("parallel", "arbitrary"),
        name="mix_out",
    )(x, a, gate_a, u, u_halo, prefix, gate_b, w_pool, pool_scale.reshape(1, d_pool), w_out,
      final_g.reshape(1, d))


def kernel(x_prompt, x_sample, cache_k, cache_v, state_pool, page_table, norm_g, w_in, sb_bias,
           w_pool, pool_scale, w_out, final_g):
    depth = norm_g.shape[0]
    bp, tp, d_model = x_prompt.shape
    bs, ts, _ = x_sample.shape
    n_heads = cache_k.shape[3]
    d_attn = n_heads * HEAD_DIM
    d_pool = state_pool.shape[-1]
    n_pool = cache_k.shape[1]
    past = page_table.shape[1] * PAGE_SIZE

    w_in_b = w_in.astype(BF16)
    wkvt_b = jnp.swapaxes(w_in_b[:, :, d_attn:3 * d_attn], 1, 2)
    w_pool_b = w_pool.astype(BF16)
    w_out_b = w_out.astype(BF16)
    kt_cache = cache_k.transpose(0, 1, 3, 4, 2).reshape(depth, n_pool, d_attn, PAGE_SIZE)
    vt_cache = cache_v.transpose(0, 1, 3, 4, 2).reshape(depth, n_pool, d_attn, PAGE_SIZE)

    h_p = x_prompt
    h_s = x_sample.reshape(1, bs * ts, d_model)
    zero_prefix = jnp.zeros((bp, HALO, d_pool), F32)
    per_seq = lambda z: z.reshape(bs, ts, -1)
    kv_prompt = None
    pool_prompt, k_sample, v_sample, pool_sample = [], [], [], []
    for l in range(depth):
        final = l == depth - 1
        q, kt, vt, ktb, vb, ga, u, gb = _norm_proj(
            h_p, norm_g[l], w_in_b[l], wkvt_b[l], d_attn, d_pool, tm=512, prompt=True,
            layer=l, depth=depth, kv_prev=kv_prompt)
        kv_prompt = (kt, vt)
        sq, kf, vf, sktb, svtb, sga, su, sgb = _norm_proj(
            h_s, norm_g[l], w_in_b[l], wkvt_b[l], d_attn, d_pool, tm=bs * ts, prompt=False)

        def new_page(zt):
            zt = zt.reshape(d_attn, bs, ts).transpose(1, 0, 2)
            return jnp.pad(zt, ((0, 0), (0, 0), (0, PAGE_SIZE - ts)))

        a_p, a_s = _attention(q, ktb, vb, per_seq(sq), new_page(sktb), new_page(svtb), sb_bias[l],
                              kt_cache, vt_cache, l, page_table)
        h_p = _mix_out(h_p, a_p, ga, u, zero_prefix, gb, w_pool_b[l], pool_scale[l], w_out_b[l],
                       final_g, tt=512, pos0=0, final=final)
        prefix = jnp.pad(state_pool[l], ((0, 0), (HALO - POOL_STATE_LEN, 0), (0, 0)))
        h_s = _mix_out(per_seq(h_s), a_s, per_seq(sga), per_seq(su), prefix, per_seq(sgb),
                       w_pool_b[l], pool_scale[l], w_out_b[l], final_g, tt=ts, pos0=past,
                       final=final).reshape(1, bs * ts, d_model)
        pool_prompt.append(u[:, tp - POOL_STATE_LEN:])
        k_sample.append(kf.reshape(bs, ts, n_heads, HEAD_DIM))
        v_sample.append(vf.reshape(bs, ts, n_heads, HEAD_DIM))
        pool_sample.append(
            jnp.concatenate([state_pool[l], per_seq(su)], axis=1)[:, -POOL_STATE_LEN:])

    def token_major(zt):
        return zt.reshape(depth, bp, n_heads, HEAD_DIM, tp).transpose(0, 1, 4, 2, 3)

    return (h_p, h_s.reshape(bs, ts, d_model), token_major(kv_prompt[0]), token_major(kv_prompt[1]),
            jnp.stack(pool_prompt), jnp.stack(k_sample), jnp.stack(v_sample), jnp.stack(pool_sample))
```

```python
import functools
import math

import jax
import jax.numpy as jnp
from jax import lax
from jax.experimental import pallas as pl
from jax.experimental.pallas import tpu as pltpu

F32 = jnp.float32
BF16 = jnp.bfloat16

PAGE_SIZE = 128
HEAD_DIM = 64
POOL_WINDOWS = (2, 4, 8, 16)
POOL_GROUP_DIM = 128
POOL_STATE_LEN = max(POOL_WINDOWS) - 1
HALO = POOL_STATE_LEN + 1
RMS_EPS = 1e-6
SB_SCALE = 1.0 / math.sqrt(HEAD_DIM)
LOG2_E = math.log2(math.e)

VMEM_LIMIT_BYTES = 56 * 1024 * 1024

_NT = (((1,), (1,)), ((), ()))


def _params(*sem):
    return pltpu.CompilerParams(dimension_semantics=sem, vmem_limit_bytes=VMEM_LIMIT_BYTES)


def _norm_proj_kernel(x_ref, g_ref, w_ref, wkvt_ref, *rest, d_attn, d_pool, prompt, layer):
    q_ref, k_ref, v_ref, kb_ref, vb_ref, ga_ref, u_ref, gb_ref = rest[-8:]
    x = x_ref[0]
    ms = jnp.mean(x * x, axis=-1, keepdims=True)
    h = (x * lax.rsqrt(ms + RMS_EPS) * g_ref[...]).astype(BF16)

    def proj(lo, width):
        return jnp.dot(h, w_ref[:, lo:lo + width], preferred_element_type=F32)

    q_ref[0] = (proj(0, d_attn) * -SB_SCALE).astype(BF16)
    kvt = lax.dot_general(wkvt_ref[...], h, _NT, preferred_element_type=F32)
    kb_ref[0] = kvt[:d_attn].astype(BF16)
    if prompt:
        if len(k_ref.shape) == 4:
            for l in range(k_ref.shape[0]):
                k_ref[l, 0] = kvt[:d_attn] if l == layer else jnp.zeros_like(kvt[:d_attn])
                v_ref[l, 0] = kvt[d_attn:] if l == layer else jnp.zeros_like(kvt[d_attn:])
        else:
            k_ref[0] = kvt[:d_attn]
            v_ref[0] = kvt[d_attn:]
        vb_ref[0] = proj(2 * d_attn, d_attn).astype(BF16)
    else:
        k_ref[0] = proj(d_attn, d_attn)
        v_ref[0] = proj(2 * d_attn, d_attn)
        vb_ref[0] = kvt[d_attn:].astype(BF16)
    ga_ref[0] = proj(3 * d_attn, d_attn)
    u_ref[0] = proj(4 * d_attn, d_pool)
    gb_ref[0] = proj(4 * d_attn + d_pool, d_pool)


def _norm_proj(x, g, w_bf16, wkvt_bf16, d_attn, d_pool, tm, prompt, layer=0, depth=1, kv_prev=None):
    b, t, d = x.shape
    d_in = w_bf16.shape[1]
    tok = lambda width: pl.BlockSpec((1, tm, width), lambda bi, i: (bi, i, 0))
    feat = pl.BlockSpec((1, d_attn, tm), lambda bi, i: (bi, 0, i))
    tok_shape = lambda width, dt: jax.ShapeDtypeStruct((b, t, width), dt)
    feat_shape = lambda dt: jax.ShapeDtypeStruct((b, d_attn, t), dt)
    if prompt:
        if kv_prev is None:
            layered = pl.BlockSpec((depth, 1, d_attn, tm), lambda bi, i: (0, bi, 0, i))
        else:
            layered = pl.BlockSpec((None, 1, d_attn, tm), lambda bi, i: (layer, bi, 0, i))
        layered_shape = jax.ShapeDtypeStruct((depth, b, d_attn, t), F32)
        kv_specs = [layered, layered, feat, tok(d_attn)]
        kv_shapes = [layered_shape, layered_shape, feat_shape(BF16), tok_shape(d_attn, BF16)]
    else:
        kv_specs = [tok(d_attn), tok(d_attn), feat, feat]
        kv_shapes = [tok_shape(d_attn, F32), tok_shape(d_attn, F32), feat_shape(BF16), feat_shape(BF16)]
    prev = list(kv_prev) if kv_prev is not None else []
    n_in = 4
    return pl.pallas_call(
        functools.partial(_norm_proj_kernel, d_attn=d_attn, d_pool=d_pool, prompt=prompt,
                          layer=layer),
        grid=(b, t // tm),
        in_specs=[
            tok(d),
            pl.BlockSpec((1, d), lambda bi, i: (0, 0)),
            pl.BlockSpec((d, d_in), lambda bi, i: (0, 0)),
            pl.BlockSpec((2 * d_attn, d), lambda bi, i: (0, 0)),
        ] + [pl.BlockSpec(memory_space=pl.ANY)] * len(prev),
        out_specs=[tok(d_attn)] + kv_specs + [tok(d_attn), tok(d_pool), tok(d_pool)],
        out_shape=[tok_shape(d_attn, BF16)] + kv_shapes
        + [tok_shape(d_attn, F32), tok_shape(d_pool, F32), tok_shape(d_pool, F32)],
        input_output_aliases={n_in + n: 1 + n for n in range(len(prev))},
        compiler_params=_params("parallel", "parallel"),
        name="norm_proj",
    )(x, g.reshape(1, d), w_bf16, wkvt_bf16, *prev)


def _log_keep(nz, mask):
    e = jnp.exp2(jnp.abs(nz) * -LOG2_E)
    log_keep = jnp.minimum(nz, 0.0) - jnp.log(1.0 + e)
    if mask is not None:
        log_keep = jnp.where(mask, log_keep, 0.0)
    return log_keep


def _suffix_matrix(n):
    j = lax.broadcasted_iota(jnp.int32, (2 * n, n), 0) % n
    s = lax.broadcasted_iota(jnp.int32, (2 * n, n), 1)
    return jnp.where(j >= s, 1.0, 0.0).astype(BF16)


def _suffix_sums(x, suffix):
    hi = x.astype(BF16)
    lo = (x - hi.astype(F32)).astype(BF16)
    return jnp.dot(jnp.concatenate([hi, lo], axis=1), suffix, preferred_element_type=F32)


def _sb_weights(nz, incl, c, mask):
    w = jnp.exp(incl + c - nz)
    if mask is not None:
        w = jnp.where(mask, w, 0.0)
    return w.astype(BF16)


SAMPLE_PAGES_PER_VISIT = 16


def _prompt_part(bias_ref, q_ref, kt_ref, v_ref, o_ref, c_ref, acc_ref, *, hp, i, blk,
                 heads_per_step, between):
    suffix = _suffix_matrix(blk)
    row = lax.broadcasted_iota(jnp.int32, (blk, blk), 0)
    col = lax.broadcasted_iota(jnp.int32, (blk, blk), 1)
    diag_mask = col < row
    feats = [slice(hh * HEAD_DIM, (hh + 1) * HEAD_DIM) for hh in range(heads_per_step)]
    heads = range(heads_per_step)
    extra = lax.broadcasted_iota(jnp.int32, (blk, HEAD_DIM), 1)
    ones = jnp.where(extra < 3, 1.0, 0.0).astype(BF16)
    qn = [jnp.concatenate([q_ref[0, :, f], ones], axis=1) for f in feats]
    term_row = lax.broadcasted_iota(jnp.int32, (HEAD_DIM, blk), 0)
    bias_rows = []
    for hh in heads:
        rest = jnp.full((HEAD_DIM, blk), -bias_ref[hp * heads_per_step + hh], F32)
        rows = jnp.zeros((HEAD_DIM, blk), F32)
        for term in range(3):
            piece = rest.astype(BF16).astype(F32)
            rows = jnp.where(term_row == term, piece, rows)
            rest = rest - piece
        bias_rows.append(rows.astype(BF16))

    def block(kb, mask):
        keys = pl.ds(pl.multiple_of(kb * blk, blk), blk)
        nz = [jnp.dot(qn[h], jnp.concatenate([kt_ref[0, feats[h], keys], bias_rows[h]], axis=0),
                      preferred_element_type=F32) for h in heads]
        log_keep = [_log_keep(nz[h], mask) for h in heads]
        incl = [_suffix_sums(log_keep[h], suffix) for h in heads]
        for h in heads:
            c = c_ref[h]
            w = _sb_weights(nz[h], incl[h], jnp.concatenate([c] * (blk // 128), axis=1), mask)
            acc_ref[h] += jnp.dot(w, v_ref[0, keys, feats[h]], preferred_element_type=F32)
            c_ref[h] = c + jnp.broadcast_to(incl[h][:, :1], c.shape)

    c_ref[...] = jnp.zeros_like(c_ref)
    acc_ref[...] = jnp.zeros_like(acc_ref)
    block(i, diag_mask)
    between()

    def body(n, _):
        block(i - 1 - n, None)
        return 0

    lax.fori_loop(0, i, body, 0)
    o_ref[0] = jnp.concatenate([acc_ref[h] for h in heads], axis=-1)


def _sample_init(bias_ref, q_ref, qbd_ref, bias_scr, c_ref, acc_ref, *, n_heads, t_new):
    rows_all, d_attn = qbd_ref.shape
    r = lax.broadcasted_iota(jnp.int32, (rows_all, d_attn), 0)
    lane = lax.broadcasted_iota(jnp.int32, (rows_all, d_attn), 1)
    q_rep = jnp.concatenate([q_ref[0].astype(F32)] * n_heads, axis=0)
    qbd_ref[...] = jnp.where(r // t_new == lane // HEAD_DIM, q_rep, 0.0).astype(BF16)
    rb = lax.broadcasted_iota(jnp.int32, (rows_all, 1), 0) // t_new
    bias_col = jnp.zeros((rows_all, 1), F32)
    for h in range(n_heads):
        bias_col = jnp.where(rb == h, bias_ref[h], bias_col)
    bias_scr[...] = bias_col
    c_ref[...] = jnp.zeros_like(c_ref)
    acc_ref[...] = jnp.zeros_like(acc_ref)


def _sample_visit(kts, vts, mask, qbd_ref, bias_scr, c_ref, acc_ref):
    rows_all = qbd_ref.shape[0]
    suffix = _suffix_matrix(PAGE_SIZE)
    cols = [slice(p * PAGE_SIZE, (p + 1) * PAGE_SIZE) for p in range(len(kts))]
    nz = jnp.dot(qbd_ref[...], jnp.concatenate(kts, axis=1),
                 preferred_element_type=F32) - bias_scr[...]
    log_keep = _log_keep(nz, mask)
    incl = _suffix_sums(jnp.concatenate([log_keep[:, s] for s in cols], axis=0), suffix)
    c = c_ref[...]
    ws = []
    for p, s in enumerate(cols):
        incl_p = incl[p * rows_all:(p + 1) * rows_all]
        ws.append(_sb_weights(nz[:, s], incl_p, c, mask))
        c = c + incl_p[:, :1]
    c_ref[...] = c
    acc_ref[...] += lax.dot_general(jnp.concatenate(ws, axis=1), jnp.concatenate(vts, axis=1),
                                    _NT, preferred_element_type=F32)


def _page_copies(pt_ref, cache_refs, page_bufs, sem, *, layer, step, slot, n_pages,
                 pages_per_step, chunks_per_seq, lookup):
    seq = step // chunks_per_seq
    newest = n_pages - 1 - (step % chunks_per_seq) * pages_per_step
    copies = []
    for p in range(pages_per_step):
        page = pt_ref[seq * n_pages + newest - p] if lookup else 0
        for which, (cache, buf) in enumerate(zip(cache_refs, page_bufs)):
            copies.append(pltpu.make_async_copy(cache.at[layer, page], buf.at[slot, p],
                                                sem.at[which, slot]))
    return copies


def _attn_kernel(pt_ref, bias_ref, q_ref, kt_ref, v_ref, sq_ref, knt_ref, vnt_ref, ktc_ref,
                 vtc_ref, o_ref, so_ref, c_ref, acc_ref, qbd_ref, bias_scr, sc_ref, sacc_ref,
                 kbuf, vbuf, sem, *, blk, heads_per_step, pages_per_step, chunks_per_seq,
                 n_heads, t_new, layer, n_pages):
    hp = pl.program_id(1)
    i = pl.program_id(2)
    n_steps = pl.num_programs(0) * pl.num_programs(1) * pl.num_programs(2)
    step = (pl.program_id(0) * pl.num_programs(1) + hp) * pl.num_programs(2) + i
    chunk = step % chunks_per_seq
    d_attn = n_heads * HEAD_DIM
    sample_state = (qbd_ref, bias_scr, sc_ref, sacc_ref)

    slot = step % 2
    copies = functools.partial(
        _page_copies, pt_ref, (ktc_ref, vtc_ref), (kbuf, vbuf), sem, layer=layer,
        n_pages=n_pages, pages_per_step=pages_per_step, chunks_per_seq=chunks_per_seq)

    @pl.when(step == 0)
    def _():
        for cp in copies(step=step, slot=slot, lookup=True):
            cp.start()

    for cp in copies(step=step, slot=slot, lookup=False):
        cp.wait()

    @pl.when(step + 1 < n_steps)
    def _():
        for cp in copies(step=step + 1, slot=1 - slot, lookup=True):
            cp.start()

    @pl.when(chunk == 0)
    def _():
        _sample_init(bias_ref, sq_ref, *sample_state, n_heads=n_heads, t_new=t_new)
        rows_all = n_heads * t_new
        t_row = lax.broadcasted_iota(jnp.int32, (rows_all, PAGE_SIZE), 0) % t_new
        s_col = lax.broadcasted_iota(jnp.int32, (rows_all, PAGE_SIZE), 1)
        _sample_visit([knt_ref[0]], [vnt_ref[0]], s_col < t_row, *sample_state)

    def past_pages():
        for lo in range(0, pages_per_step, SAMPLE_PAGES_PER_VISIT):
            pages = range(lo, lo + SAMPLE_PAGES_PER_VISIT)
            _sample_visit([kbuf[slot, p].astype(BF16) for p in pages],
                          [vbuf[slot, p].astype(BF16) for p in pages], None, *sample_state)

    _prompt_part(bias_ref, q_ref, kt_ref, v_ref, o_ref, c_ref, acc_ref, hp=hp, i=i, blk=blk,
                 heads_per_step=heads_per_step, between=past_pages)

    @pl.when(chunk == chunks_per_seq - 1)
    def _():
        acc = sacc_ref[...]
        lane = lax.broadcasted_iota(jnp.int32, (t_new, d_attn), 1) // HEAD_DIM
        out = jnp.zeros((t_new, d_attn), F32)
        for h in range(n_heads):
            out = jnp.where(lane == h, acc[h * t_new:(h + 1) * t_new, :], out)
        so_ref[0] = out


def _attention(q, kt, v, sq, knt, vnt, bias, kt_cache, vt_cache, layer, page_table, blk=256,
               heads_per_step=4):
    b, t, d_attn = q.shape
    bs, t_new, _ = sq.shape
    n_heads = d_attn // HEAD_DIM
    n_pages = page_table.shape[1]
    width = heads_per_step * HEAD_DIM
    grid = (b, d_attn // width, t // blk)
    n_steps = grid[0] * grid[1] * grid[2]
    chunks_per_seq, rem = divmod(n_steps, bs)
    pages_per_step, rem2 = divmod(n_pages, chunks_per_seq)
    assert rem == 0 and rem2 == 0 and pages_per_step % SAMPLE_PAGES_PER_VISIT == 0
    rows_all = n_heads * t_new

    def step_of(bi, hp, i):
        return (bi * grid[1] + hp) * grid[2] + i

    per_seq = lambda rows, cols: pl.BlockSpec(
        (1, rows, cols), lambda bi, hp, i, pt: (step_of(bi, hp, i) // chunks_per_seq, 0, 0))
    page_buf = pltpu.VMEM((2, pages_per_step, d_attn, PAGE_SIZE), F32)
    return pl.pallas_call(
        functools.partial(_attn_kernel, blk=blk, heads_per_step=heads_per_step,
                          pages_per_step=pages_per_step, chunks_per_seq=chunks_per_seq,
                          n_heads=n_heads, t_new=t_new, layer=layer, n_pages=n_pages),
        grid_spec=pltpu.PrefetchScalarGridSpec(
            num_scalar_prefetch=1,
            grid=grid,
            in_specs=[
                pl.BlockSpec(memory_space=pltpu.SMEM),
                pl.BlockSpec((1, blk, width), lambda bi, hp, i, pt: (bi, i, hp)),
                pl.BlockSpec((1, width, t), lambda bi, hp, i, pt: (bi, hp, 0)),
                pl.BlockSpec((1, t, width), lambda bi, hp, i, pt: (bi, 0, hp)),
                per_seq(t_new, d_attn), per_seq(d_attn, PAGE_SIZE), per_seq(d_attn, PAGE_SIZE),
                pl.BlockSpec(memory_space=pl.ANY), pl.BlockSpec(memory_space=pl.ANY),
            ],
            out_specs=[pl.BlockSpec((1, blk, width), lambda bi, hp, i, pt: (bi, i, hp)),
                       per_seq(t_new, d_attn)],
            scratch_shapes=[
                pltpu.VMEM((heads_per_step, blk, 128), F32),
                pltpu.VMEM((heads_per_step, blk, HEAD_DIM), F32),
                pltpu.VMEM((rows_all, d_attn), BF16),
                pltpu.VMEM((rows_all, 1), F32),
                pltpu.VMEM((rows_all, 1), F32),
                pltpu.VMEM((rows_all, d_attn), F32),
                page_buf, page_buf,
                pltpu.SemaphoreType.DMA((2, 2)),
            ],
        ),
        out_shape=[jax.ShapeDtypeStruct((b, t, d_attn), F32),
                   jax.ShapeDtypeStruct((bs, t_new, d_attn), F32)],
        compiler_params=_params("arbitrary", "arbitrary", "arbitrary"),
        name="sb_attn",
    )(page_table.reshape(-1), bias, q, kt, v, sq, knt, vnt, kt_cache, vt_cache)


def _silu(x):
    return x * (1.0 / (1.0 + jnp.exp(-x)))


def _mix_out_kernel(x_ref, a_ref, ga_ref, u_ref, uh_ref, pre_ref, gb_ref, wp_ref, ps_ref,
                    wo_ref, fg_ref, o_ref, seq_ref, *, tt, pos0, final):
    j = pl.program_id(1)
    seq_ref[0:HALO, :] = jnp.where(j == 0, pre_ref[0], uh_ref[0])
    seq_ref[HALO:, :] = u_ref[0]
    pos = pos0 + j * tt + lax.broadcasted_iota(jnp.int32, (tt, 1), 0)
    gb = gb_ref[0]
    parts = [a_ref[0] * _silu(ga_ref[0])]
    for g, w in enumerate(POOL_WINDOWS):
        lanes = slice(g * POOL_GROUP_DIM, (g + 1) * POOL_GROUP_DIM)
        win_sum = seq_ref[HALO:HALO + tt, lanes]
        for back in range(1, w):
            win_sum = win_sum + seq_ref[HALO - back:HALO - back + tt, lanes]
        count = jnp.minimum(pos + 1, w).astype(F32)
        diff = win_sum / count - seq_ref[HALO:HALO + tt, lanes]
        y = jnp.dot(diff.astype(BF16), wp_ref[g], preferred_element_type=F32) * ps_ref[:, lanes]
        parts.append(y * _silu(gb[:, lanes]))
    mix = jnp.concatenate(parts, axis=-1).astype(BF16)
    h = x_ref[0] + jnp.dot(mix, wo_ref[...], preferred_element_type=F32)
    if final:
        ms = jnp.mean(h * h, axis=-1, keepdims=True)
        h = h * lax.rsqrt(ms + RMS_EPS) * fg_ref[...]
    o_ref[0] = h


def _mix_out(x, a, gate_a, u, prefix, gate_b, w_pool, pool_scale, w_out, final_g, *, tt, pos0, final):
    b, t, d = x.shape
    d_attn = a.shape[-1]
    d_pool = u.shape[-1]
    tile = lambda width: pl.BlockSpec((1, tt, width), lambda bi, j: (bi, j, 0))
    const = lambda shape: pl.BlockSpec(shape, lambda bi, j: (0,) * len(shape))
    halo_blocks = tt // HALO if tt >= HALO else 1
    halo_spec = pl.BlockSpec(
        (1, HALO, d_pool), lambda bi, j: (bi, jnp.maximum(j * halo_blocks - 1, 0), 0))
    u_halo = u if t >= HALO else prefix
    return pl.pallas_call(
        functools.partial(_mix_out_kernel, tt=tt, pos0=pos0, final=final),
        grid=(b, t // tt),
        in_specs=[
            tile(d), tile(d_attn), tile(d_attn), tile(d_pool), halo_spec,
            pl.BlockSpec((1, HALO, d_pool), lambda bi, j: (bi, 0, 0)),
            tile(d_pool),
            const(w_pool.shape), const((1, d_pool)), const(w_out.shape), const((1, d)),
        ],
        out_specs=tile(d),
        out_shape=jax.ShapeDtypeStruct((b, t, d), F32),
        scratch_shapes=[pltpu.VMEM((HALO + tt, d_pool), F32)],
        compiler_params=_params("parallel", "arbitrary"),
        name="mix_out",
    )(x, a, gate_a, u, u_halo, prefix, gate_b, w_pool, pool_scale.reshape(1, d_pool), w_out,
      final_g.reshape(1, d))


def kernel(x_prompt, x_sample, cache_k, cache_v, state_pool, page_table, norm_g, w_in, sb_bias,
           w_pool, pool_scale, w_out, final_g):
    depth = norm_g.shape[0]
    bp, tp, d_model = x_prompt.shape
    bs, ts, _ = x_sample.shape
    n_heads = cache_k.shape[3]
    d_attn = n_heads * HEAD_DIM
    d_pool = state_pool.shape[-1]
    n_pool = cache_k.shape[1]
    past = page_table.shape[1] * PAGE_SIZE

    w_in_b = w_in.astype(BF16)
    wkvt_b = jnp.swapaxes(w_in_b[:, :, d_attn:3 * d_attn], 1, 2)
    w_pool_b = w_pool.astype(BF16)
    w_out_b = w_out.astype(BF16)
    kt_cache = cache_k.transpose(0, 1, 3, 4, 2).reshape(depth, n_pool, d_attn, PAGE_SIZE)
    vt_cache = cache_v.transpose(0, 1, 3, 4, 2).reshape(depth, n_pool, d_attn, PAGE_SIZE)

    h_p = x_prompt
    h_s = x_sample.reshape(1, bs * ts, d_model)
    zero_prefix = jnp.zeros((bp, HALO, d_pool), F32)
    per_seq = lambda z: z.reshape(bs, ts, -1)
    kv_prompt = None
    pool_prompt, k_sample, v_sample, pool_sample = [], [], [], []
    for l in range(depth):
        final = l == depth - 1
        q, kt, vt, ktb, vb, ga, u, gb = _norm_proj(
            h_p, norm_g[l], w_in_b[l], wkvt_b[l], d_attn, d_pool, tm=512, prompt=True,
            layer=l, depth=depth, kv_prev=kv_prompt)
        kv_prompt = (kt, vt)
        sq, kf, vf, sktb, svtb, sga, su, sgb = _norm_proj(
            h_s, norm_g[l], w_in_b[l], wkvt_b[l], d_attn, d_pool, tm=bs * ts, prompt=False)

        def new_page(zt):
            zt = zt.reshape(d_attn, bs, ts).transpose(1, 0, 2)
            return jnp.pad(zt, ((0, 0), (0, 0), (0, PAGE_SIZE - ts)))

        a_p, a_s = _attention(q, ktb, vb, per_seq(sq), new_page(sktb), new_page(svtb), sb_bias[l],
                              kt_cache, vt_cache, l, page_table)
        h_p = _mix_out(h_p, a_p, ga, u, zero_prefix, gb, w_pool_b[l], pool_scale[l], w_out_b[l],
                       final_g, tt=512, pos0=0, final=final)
        prefix = jnp.pad(state_pool[l], ((0, 0), (HALO - POOL_STATE_LEN, 0), (0, 0)))
        h_s = _mix_out(per_seq(h_s), a_s, per_seq(sga), per_seq(su), prefix, per_seq(sgb),
                       w_pool_b[l], pool_scale[l], w_out_b[l], final_g, tt=ts, pos0=past,
                       final=final).reshape(1, bs * ts, d_model)
        pool_prompt.append(u[:, tp - POOL_STATE_LEN:])
        k_sample.append(kf.reshape(bs, ts, n_heads, HEAD_DIM))
        v_sample.append(vf.reshape(bs, ts, n_heads, HEAD_DIM))
        pool_sample.append(
            jnp.concatenate([state_pool[l], per_seq(su)], axis=1)[:, -POOL_STATE_LEN:])

    def token_major(zt):
        return zt.reshape(depth, bp, n_heads, HEAD_DIM, tp).transpose(0, 1, 4, 2, 3)

    return (h_p, h_s.reshape(bs, ts, d_model), token_major(kv_prompt[0]), token_major(kv_prompt[1]),
            jnp.stack(pool_prompt), jnp.stack(k_sample), jnp.stack(v_sample), jnp.stack(pool_sample))
```

```python
import functools
import math

import jax
import jax.numpy as jnp
from jax import lax
from jax.experimental import pallas as pl
from jax.experimental.pallas import tpu as pltpu

F32 = jnp.float32
BF16 = jnp.bfloat16

PAGE_SIZE = 128
HEAD_DIM = 64
POOL_WINDOWS = (2, 4, 8, 16)
POOL_GROUP_DIM = 128
POOL_STATE_LEN = max(POOL_WINDOWS) - 1
HALO = POOL_STATE_LEN + 1
RMS_EPS = 1e-6
SB_SCALE = 1.0 / math.sqrt(HEAD_DIM)
LOG2_E = math.log2(math.e)

VMEM_LIMIT_BYTES = 56 * 1024 * 1024

_NT = (((1,), (1,)), ((), ()))


def _params(*sem):
    return pltpu.CompilerParams(dimension_semantics=sem, vmem_limit_bytes=VMEM_LIMIT_BYTES)


def _norm_proj_kernel(x_ref, g_ref, w_ref, wkvt_ref, *rest, d_attn, d_pool, prompt, layer):
    q_ref, k_ref, v_ref, kb_ref, vb_ref, ga_ref, u_ref, gb_ref = rest[-8:]
    x = x_ref[0]
    ms = jnp.mean(x * x, axis=-1, keepdims=True)
    h = (x * lax.rsqrt(ms + RMS_EPS) * g_ref[...]).astype(BF16)

    def proj(lo, width):
        return jnp.dot(h, w_ref[:, lo:lo + width], preferred_element_type=F32)

    q_ref[0] = (proj(0, d_attn) * -SB_SCALE).astype(BF16)
    kvt = lax.dot_general(wkvt_ref[...], h, _NT, preferred_element_type=F32)
    kb_ref[0] = kvt[:d_attn].astype(BF16)
    if prompt:
        if len(k_ref.shape) == 4:
            for l in range(k_ref.shape[0]):
                k_ref[l, 0] = kvt[:d_attn] if l == layer else jnp.zeros_like(kvt[:d_attn])
                v_ref[l, 0] = kvt[d_attn:] if l == layer else jnp.zeros_like(kvt[d_attn:])
        else:
            k_ref[0] = kvt[:d_attn]
            v_ref[0] = kvt[d_attn:]
        vb_ref[0] = proj(2 * d_attn, d_attn).astype(BF16)
    else:
        k_ref[0] = proj(d_attn, d_attn)
        v_ref[0] = proj(2 * d_attn, d_attn)
        vb_ref[0] = kvt[d_attn:].astype(BF16)
    ga_ref[0] = proj(3 * d_attn, d_attn)
    u_ref[0] = proj(4 * d_attn, d_pool)
    gb_ref[0] = proj(4 * d_attn + d_pool, d_pool)


def _norm_proj(x, g, w_bf16, wkvt_bf16, d_attn, d_pool, tm, prompt, layer=0, depth=1, kv_prev=None):
    b, t, d = x.shape
    d_in = w_bf16.shape[1]
    tok = lambda width: pl.BlockSpec((1, tm, width), lambda bi, i: (bi, i, 0))
    feat = pl.BlockSpec((1, d_attn, tm), lambda bi, i: (bi, 0, i))
    tok_shape = lambda width, dt: jax.ShapeDtypeStruct((b, t, width), dt)
    feat_shape = lambda dt: jax.ShapeDtypeStruct((b, d_attn, t), dt)
    if prompt:
        if kv_prev is None:
            layered = pl.BlockSpec((depth, 1, d_attn, tm), lambda bi, i: (0, bi, 0, i))
        else:
            layered = pl.BlockSpec((None, 1, d_attn, tm), lambda bi, i: (layer, bi, 0, i))
        layered_shape = jax.ShapeDtypeStruct((depth, b, d_attn, t), F32)
        kv_specs = [layered, layered, feat, tok(d_attn)]
        kv_shapes = [layered_shape, layered_shape, feat_shape(BF16), tok_shape(d_attn, BF16)]
    else:
        kv_specs = [tok(d_attn), tok(d_attn), feat, feat]
        kv_shapes = [tok_shape(d_attn, F32), tok_shape(d_attn, F32), feat_shape(BF16), feat_shape(BF16)]
    prev = list(kv_prev) if kv_prev is not None else []
    n_in = 4
    return pl.pallas_call(
        functools.partial(_norm_proj_kernel, d_attn=d_attn, d_pool=d_pool, prompt=prompt,
                          layer=layer),
        grid=(b, t // tm),
        in_specs=[
            tok(d),
            pl.BlockSpec((1, d), lambda bi, i: (0, 0)),
            pl.BlockSpec((d, d_in), lambda bi, i: (0, 0)),
            pl.BlockSpec((2 * d_attn, d), lambda bi, i: (0, 0)),
        ] + [pl.BlockSpec(memory_space=pl.ANY)] * len(prev),
        out_specs=[tok(d_attn)] + kv_specs + [tok(d_attn), tok(d_pool), tok(d_pool)],
        out_shape=[tok_shape(d_attn, BF16)] + kv_shapes
        + [tok_shape(d_attn, F32), tok_shape(d_pool, F32), tok_shape(d_pool, F32)],
        input_output_aliases={n_in + n: 1 + n for n in range(len(prev))},
        compiler_params=_params("parallel", "parallel"),
        name="norm_proj",
    )(x, g.reshape(1, d), w_bf16, wkvt_bf16, *prev)


def _log_keep(nz, mask):
    e = jnp.exp2(jnp.abs(nz) * -LOG2_E)
    log_keep = jnp.minimum(nz, 0.0) - jnp.log(1.0 + e)
    if mask is not None:
        log_keep = jnp.where(mask, log_keep, 0.0)
    return log_keep


def _suffix_matrix(n):
    j = lax.broadcasted_iota(jnp.int32, (n, n), 0)
    s = lax.broadcasted_iota(jnp.int32, (n, n), 1)
    return jnp.where(j >= s, 1.0, 0.0).astype(BF16)


def _suffix_sums(x, suffix):
    return jnp.dot(x.astype(BF16), suffix, preferred_element_type=F32)


def _sb_weights(nz, incl, c, mask):
    w = jnp.exp(incl + c - nz)
    if mask is not None:
        w = jnp.where(mask, w, 0.0)
    return w.astype(BF16)


SAMPLE_PAGES_PER_VISIT = 16


def _prompt_part(bias_ref, q_ref, kt_ref, v_ref, o_ref, c_ref, acc_ref, *, hp, i, blk,
                 heads_per_step, between):
    suffix = _suffix_matrix(blk)
    row = lax.broadcasted_iota(jnp.int32, (blk, blk), 0)
    col = lax.broadcasted_iota(jnp.int32, (blk, blk), 1)
    diag_mask = col < row
    feats = [slice(hh * HEAD_DIM, (hh + 1) * HEAD_DIM) for hh in range(heads_per_step)]
    heads = range(heads_per_step)
    extra = lax.broadcasted_iota(jnp.int32, (blk, HEAD_DIM), 1)
    ones = jnp.where(extra < 3, 1.0, 0.0).astype(BF16)
    qn = [jnp.concatenate([q_ref[0, :, f], ones], axis=1) for f in feats]
    term_row = lax.broadcasted_iota(jnp.int32, (HEAD_DIM, blk), 0)
    bias_rows = []
    for hh in heads:
        rest = jnp.full((HEAD_DIM, blk), -bias_ref[hp * heads_per_step + hh], F32)
        rows = jnp.zeros((HEAD_DIM, blk), F32)
        for term in range(3):
            piece = rest.astype(BF16).astype(F32)
            rows = jnp.where(term_row == term, piece, rows)
            rest = rest - piece
        bias_rows.append(rows.astype(BF16))

    def block(kb, mask):
        keys = pl.ds(pl.multiple_of(kb * blk, blk), blk)
        nz = [jnp.dot(qn[h], jnp.concatenate([kt_ref[0, feats[h], keys], bias_rows[h]], axis=0),
                      preferred_element_type=F32) for h in heads]
        log_keep = [_log_keep(nz[h], mask) for h in heads]
        incl = [_suffix_sums(log_keep[h], suffix) for h in heads]
        for h in heads:
            c = c_ref[h]
            w = _sb_weights(nz[h], incl[h], jnp.concatenate([c] * (blk // 128), axis=1), mask)
            acc_ref[h] += jnp.dot(w, v_ref[0, keys, feats[h]], preferred_element_type=F32)
            c_ref[h] = c + jnp.broadcast_to(incl[h][:, :1], c.shape)

    c_ref[...] = jnp.zeros_like(c_ref)
    acc_ref[...] = jnp.zeros_like(acc_ref)
    block(i, diag_mask)
    between()

    def body(n, _):
        block(i - 1 - n, None)
        return 0

    lax.fori_loop(0, i, body, 0)
    o_ref[0] = jnp.concatenate([acc_ref[h] for h in heads], axis=-1)


def _sample_init(bias_ref, q_ref, qbd_ref, bias_scr, c_ref, acc_ref, *, n_heads, t_new):
    rows_all, d_attn = qbd_ref.shape
    r = lax.broadcasted_iota(jnp.int32, (rows_all, d_attn), 0)
    lane = lax.broadcasted_iota(jnp.int32, (rows_all, d_attn), 1)
    q_rep = jnp.concatenate([q_ref[0].astype(F32)] * n_heads, axis=0)
    qbd_ref[...] = jnp.where(r // t_new == lane // HEAD_DIM, q_rep, 0.0).astype(BF16)
    rb = lax.broadcasted_iota(jnp.int32, (rows_all, 1), 0) // t_new
    bias_col = jnp.zeros((rows_all, 1), F32)
    for h in range(n_heads):
        bias_col = jnp.where(rb == h, bias_ref[h], bias_col)
    bias_scr[...] = bias_col
    c_ref[...] = jnp.zeros_like(c_ref)
    acc_ref[...] = jnp.zeros_like(acc_ref)


def _sample_visit(kts, vts, mask, qbd_ref, bias_scr, c_ref, acc_ref):
    rows_all = qbd_ref.shape[0]
    suffix = _suffix_matrix(PAGE_SIZE)
    cols = [slice(p * PAGE_SIZE, (p + 1) * PAGE_SIZE) for p in range(len(kts))]
    nz = jnp.dot(qbd_ref[...], jnp.concatenate(kts, axis=1),
                 preferred_element_type=F32) - bias_scr[...]
    log_keep = _log_keep(nz, mask)
    incl = _suffix_sums(jnp.concatenate([log_keep[:, s] for s in cols], axis=0), suffix)
    c = c_ref[...]
    ws = []
    for p, s in enumerate(cols):
        incl_p = incl[p * rows_all:(p + 1) * rows_all]
        ws.append(_sb_weights(nz[:, s], incl_p, c, mask))
        c = c + incl_p[:, :1]
    c_ref[...] = c
    acc_ref[...] += lax.dot_general(jnp.concatenate(ws, axis=1), jnp.concatenate(vts, axis=1),
                                    _NT, preferred_element_type=F32)


def _page_copies(pt_ref, cache_refs, page_bufs, sem, *, layer, step, slot, n_pages,
                 pages_per_step, chunks_per_seq, lookup):
    seq = step // chunks_per_seq
    newest = n_pages - 1 - (step % chunks_per_seq) * pages_per_step
    copies = []
    for p in range(pages_per_step):
        page = pt_ref[seq * n_pages + newest - p] if lookup else 0
        for which, (cache, buf) in enumerate(zip(cache_refs, page_bufs)):
            copies.append(pltpu.make_async_copy(cache.at[layer, page], buf.at[slot, p],
                                                sem.at[which, slot]))
    return copies


def _attn_kernel(pt_ref, bias_ref, q_ref, kt_ref, v_ref, sq_ref, knt_ref, vnt_ref, ktc_ref,
                 vtc_ref, o_ref, so_ref, c_ref, acc_ref, qbd_ref, bias_scr, sc_ref, sacc_ref,
                 kbuf, vbuf, sem, *, blk, heads_per_step, pages_per_step, chunks_per_seq,
                 n_heads, t_new, layer, n_pages):
    hp = pl.program_id(1)
    i = pl.program_id(2)
    n_steps = pl.num_programs(0) * pl.num_programs(1) * pl.num_programs(2)
    step = (pl.program_id(0) * pl.num_programs(1) + hp) * pl.num_programs(2) + i
    chunk = step % chunks_per_seq
    d_attn = n_heads * HEAD_DIM
    sample_state = (qbd_ref, bias_scr, sc_ref, sacc_ref)

    slot = step % 2
    copies = functools.partial(
        _page_copies, pt_ref, (ktc_ref, vtc_ref), (kbuf, vbuf), sem, layer=layer,
        n_pages=n_pages, pages_per_step=pages_per_step, chunks_per_seq=chunks_per_seq)

    @pl.when(step == 0)
    def _():
        for cp in copies(step=step, slot=slot, lookup=True):
            cp.start()

    for cp in copies(step=step, slot=slot, lookup=False):
        cp.wait()

    @pl.when(step + 1 < n_steps)
    def _():
        for cp in copies(step=step + 1, slot=1 - slot, lookup=True):
            cp.start()

    @pl.when(chunk == 0)
    def _():
        _sample_init(bias_ref, sq_ref, *sample_state, n_heads=n_heads, t_new=t_new)
        rows_all = n_heads * t_new
        t_row = lax.broadcasted_iota(jnp.int32, (rows_all, PAGE_SIZE), 0) % t_new
        s_col = lax.broadcasted_iota(jnp.int32, (rows_all, PAGE_SIZE), 1)
        _sample_visit([knt_ref[0]], [vnt_ref[0]], s_col < t_row, *sample_state)

    def past_pages():
        for lo in range(0, pages_per_step, SAMPLE_PAGES_PER_VISIT):
            pages = range(lo, lo + SAMPLE_PAGES_PER_VISIT)
            _sample_visit([kbuf[slot, p].astype(BF16) for p in pages],
                          [vbuf[slot, p].astype(BF16) for p in pages], None, *sample_state)

    _prompt_part(bias_ref, q_ref, kt_ref, v_ref, o_ref, c_ref, acc_ref, hp=hp, i=i, blk=blk,
                 heads_per_step=heads_per_step, between=past_pages)

    @pl.when(chunk == chunks_per_seq - 1)
    def _():
        acc = sacc_ref[...]
        lane = lax.broadcasted_iota(jnp.int32, (t_new, d_attn), 1) // HEAD_DIM
        out = jnp.zeros((t_new, d_attn), F32)
        for h in range(n_heads):
            out = jnp.where(lane == h, acc[h * t_new:(h + 1) * t_new, :], out)
        so_ref[0] = out


def _attention(q, kt, v, sq, knt, vnt, bias, kt_cache, vt_cache, layer, page_table, blk=256,
               heads_per_step=4):
    b, t, d_attn = q.shape
    bs, t_new, _ = sq.shape
    n_heads = d_attn // HEAD_DIM
    n_pages = page_table.shape[1]
    width = heads_per_step * HEAD_DIM
    grid = (b, d_attn // width, t // blk)
    n_steps = grid[0] * grid[1] * grid[2]
    chunks_per_seq, rem = divmod(n_steps, bs)
    pages_per_step, rem2 = divmod(n_pages, chunks_per_seq)
    assert rem == 0 and rem2 == 0 and pages_per_step % SAMPLE_PAGES_PER_VISIT == 0
    rows_all = n_heads * t_new

    def step_of(bi, hp, i):
        return (bi * grid[1] + hp) * grid[2] + i

    per_seq = lambda rows, cols: pl.BlockSpec(
        (1, rows, cols), lambda bi, hp, i, pt: (step_of(bi, hp, i) // chunks_per_seq, 0, 0))
    page_buf = pltpu.VMEM((2, pages_per_step, d_attn, PAGE_SIZE), F32)
    return pl.pallas_call(
        functools.partial(_attn_kernel, blk=blk, heads_per_step=heads_per_step,
                          pages_per_step=pages_per_step, chunks_per_seq=chunks_per_seq,
                          n_heads=n_heads, t_new=t_new, layer=layer, n_pages=n_pages),
        grid_spec=pltpu.PrefetchScalarGridSpec(
            num_scalar_prefetch=1,
            grid=grid,
            in_specs=[
                pl.BlockSpec(memory_space=pltpu.SMEM),
                pl.BlockSpec((1, blk, width), lambda bi, hp, i, pt: (bi, i, hp)),
                pl.BlockSpec((1, width, t), lambda bi, hp, i, pt: (bi, hp, 0)),
                pl.BlockSpec((1, t, width), lambda bi, hp, i, pt: (bi, 0, hp)),
                per_seq(t_new, d_attn), per_seq(d_attn, PAGE_SIZE), per_seq(d_attn, PAGE_SIZE),
                pl.BlockSpec(memory_space=pl.ANY), pl.BlockSpec(memory_space=pl.ANY),
            ],
            out_specs=[pl.BlockSpec((1, blk, width), lambda bi, hp, i, pt: (bi, i, hp)),
                       per_seq(t_new, d_attn)],
            scratch_shapes=[
                pltpu.VMEM((heads_per_step, blk, 128), F32),
                pltpu.VMEM((heads_per_step, blk, HEAD_DIM), F32),
                pltpu.VMEM((rows_all, d_attn), BF16),
                pltpu.VMEM((rows_all, 1), F32),
                pltpu.VMEM((rows_all, 1), F32),
                pltpu.VMEM((rows_all, d_attn), F32),
                page_buf, page_buf,
                pltpu.SemaphoreType.DMA((2, 2)),
            ],
        ),
        out_shape=[jax.ShapeDtypeStruct((b, t, d_attn), F32),
                   jax.ShapeDtypeStruct((bs, t_new, d_attn), F32)],
        compiler_params=_params("arbitrary", "arbitrary", "arbitrary"),
        name="sb_attn",
    )(page_table.reshape(-1), bias, q, kt, v, sq, knt, vnt, kt_cache, vt_cache)


def _silu(x):
    return x * (1.0 / (1.0 + jnp.exp(-x)))


def _mix_out_kernel(x_ref, a_ref, ga_ref, u_ref, uh_ref, pre_ref, gb_ref, wp_ref, ps_ref,
                    wo_ref, fg_ref, o_ref, seq_ref, *, tt, pos0, final):
    j = pl.program_id(1)
    seq_ref[0:HALO, :] = jnp.where(j == 0, pre_ref[0], uh_ref[0])
    seq_ref[HALO:, :] = u_ref[0]
    pos = pos0 + j * tt + lax.broadcasted_iota(jnp.int32, (tt, 1), 0)
    gb = gb_ref[0]
    parts = [a_ref[0] * _silu(ga_ref[0])]
    for g, w in enumerate(POOL_WINDOWS):
        lanes = slice(g * POOL_GROUP_DIM, (g + 1) * POOL_GROUP_DIM)
        win_sum = seq_ref[HALO:HALO + tt, lanes]
        for back in range(1, w):
            win_sum = win_sum + seq_ref[HALO - back:HALO - back + tt, lanes]
        count = jnp.minimum(pos + 1, w).astype(F32)
        diff = win_sum / count - seq_ref[HALO:HALO + tt, lanes]
        y = jnp.dot(diff.astype(BF16), wp_ref[g], preferred_element_type=F32) * ps_ref[:, lanes]
        parts.append(y * _silu(gb[:, lanes]))
    mix = jnp.concatenate(parts, axis=-1).astype(BF16)
    h = x_ref[0] + jnp.dot(mix, wo_ref[...], preferred_element_type=F32)
    if final:
        ms = jnp.mean(h * h, axis=-1, keepdims=True)
        h = h * lax.rsqrt(ms + RMS_EPS) * fg_ref[...]
    o_ref[0] = h


def _mix_out(x, a, gate_a, u, prefix, gate_b, w_pool, pool_scale, w_out, final_g, *, tt, pos0, final):
    b, t, d = x.shape
    d_attn = a.shape[-1]
    d_pool = u.shape[-1]
    tile = lambda width: pl.BlockSpec((1, tt, width), lambda bi, j: (bi, j, 0))
    const = lambda shape: pl.BlockSpec(shape, lambda bi, j: (0,) * len(shape))
    halo_blocks = tt // HALO if tt >= HALO else 1
    halo_spec = pl.BlockSpec(
        (1, HALO, d_pool), lambda bi, j: (bi, jnp.maximum(j * halo_blocks - 1, 0), 0))
    u_halo = u if t >= HALO else prefix
    return pl.pallas_call(
        functools.partial(_mix_out_kernel, tt=tt, pos0=pos0, final=final),
        grid=(b, t // tt),
        in_specs=[
            tile(d), tile(d_attn), tile(d_attn), tile(d_pool), halo_spec,
            pl.BlockSpec((1, HALO, d_pool), lambda bi, j: (bi, 0, 0)),
            tile(d_pool),
            const(w_pool.shape), const((1, d_pool)), const(w_out.shape), const((1, d)),
        ],
        out_specs=tile(d),
        out_shape=jax.ShapeDtypeStruct((b, t, d), F32),
        scratch_shapes=[pltpu.VMEM((HALO + tt, d_pool), F32)],
        compiler_params=_params("parallel", "arbitrary"),
        name="mix_out",
    )(x, a, gate_a, u, u_halo, prefix, gate_b, w_pool, pool_scale.reshape(1, d_pool), w_out,
      final_g.reshape(1, d))


def kernel(x_prompt, x_sample, cache_k, cache_v, state_pool, page_table, norm_g, w_in, sb_bias,
           w_pool, pool_scale, w_out, final_g):
    depth = norm_g.shape[0]
    bp, tp, d_model = x_prompt.shape
    bs, ts, _ = x_sample.shape
    n_heads = cache_k.shape[3]
    d_attn = n_heads * HEAD_DIM
    d_pool = state_pool.shape[-1]
    n_pool = cache_k.shape[1]
    past = page_table.shape[1] * PAGE_SIZE

    w_in_b = w_in.astype(BF16)
    wkvt_b = jnp.swapaxes(w_in_b[:, :, d_attn:3 * d_attn], 1, 2)
    w_pool_b = w_pool.astype(BF16)
    w_out_b = w_out.astype(BF16)
    kt_cache = cache_k.transpose(0, 1, 3, 4, 2).reshape(depth, n_pool, d_attn, PAGE_SIZE)
    vt_cache = cache_v.transpose(0, 1, 3, 4, 2).reshape(depth, n_pool, d_attn, PAGE_SIZE)

    h_p = x_prompt
    h_s = x_sample.reshape(1, bs * ts, d_model)
    zero_prefix = jnp.zeros((bp, HALO, d_pool), F32)
    per_seq = lambda z: z.reshape(bs, ts, -1)
    kv_prompt = None
    pool_prompt, k_sample, v_sample, pool_sample = [], [], [], []
    for l in range(depth):
        final = l == depth - 1
        q, kt, vt, ktb, vb, ga, u, gb = _norm_proj(
            h_p, norm_g[l], w_in_b[l], wkvt_b[l], d_attn, d_pool, tm=512, prompt=True,
            layer=l, depth=depth, kv_prev=kv_prompt)
        kv_prompt = (kt, vt)
        sq, kf, vf, sktb, svtb, sga, su, sgb = _norm_proj(
            h_s, norm_g[l], w_in_b[l], wkvt_b[l], d_attn, d_pool, tm=bs * ts, prompt=False)

        def new_page(zt):
            zt = zt.reshape(d_attn, bs, ts).transpose(1, 0, 2)
            return jnp.pad(zt, ((0, 0), (0, 0), (0, PAGE_SIZE - ts)))

        a_p, a_s = _attention(q, ktb, vb, per_seq(sq), new_page(sktb), new_page(svtb), sb_bias[l],
                              kt_cache, vt_cache, l, page_table)
        h_p = _mix_out(h_p, a_p, ga, u, zero_prefix, gb, w_pool_b[l], pool_scale[l], w_out_b[l],
                       final_g, tt=512, pos0=0, final=final)
        prefix = jnp.pad(state_pool[l], ((0, 0), (HALO - POOL_STATE_LEN, 0), (0, 0)))
        h_s = _mix_out(per_seq(h_s), a_s, per_seq(sga), per_seq(su), prefix, per_seq(sgb),
                       w_pool_b[l], pool_scale[l], w_out_b[l], final_g, tt=ts, pos0=past,
                       final=final).reshape(1, bs * ts, d_model)
        pool_prompt.append(u[:, tp - POOL_STATE_LEN:])
        k_sample.append(kf.reshape(bs, ts, n_heads, HEAD_DIM))
        v_sample.append(vf.reshape(bs, ts, n_heads, HEAD_DIM))
        pool_sample.append(
            jnp.concatenate([state_pool[l], per_seq(su)], axis=1)[:, -POOL_STATE_LEN:])

    def token_major(zt):
        return zt.reshape(depth, bp, n_heads, HEAD_DIM, tp).transpose(0, 1, 4, 2, 3)

    return (h_p, h_s.reshape(bs, ts, d_model), token_major(kv_prompt[0]), token_major(kv_prompt[1]),
            jnp.stack(pool_prompt), jnp.stack(k_sample), jnp.stack(v_sample), jnp.stack(pool_sample))
```

```python
import functools
import math

import jax
import jax.numpy as jnp
from jax import lax
from jax.experimental import pallas as pl
from jax.experimental.pallas import tpu as pltpu

F32 = jnp.float32
BF16 = jnp.bfloat16

PAGE_SIZE = 128
HEAD_DIM = 64
POOL_WINDOWS = (2, 4, 8, 16)
POOL_GROUP_DIM = 128
POOL_STATE_LEN = max(POOL_WINDOWS) - 1
HALO = POOL_STATE_LEN + 1
RMS_EPS = 1e-6
SB_SCALE = 1.0 / math.sqrt(HEAD_DIM)
LOG2_E = math.log2(math.e)

VMEM_LIMIT_BYTES = 56 * 1024 * 1024

_NT = (((1,), (1,)), ((), ()))


def _params(*sem):
    return pltpu.CompilerParams(dimension_semantics=sem, vmem_limit_bytes=VMEM_LIMIT_BYTES)


def _norm_proj_kernel(x_ref, g_ref, w_ref, wkvt_ref, *rest, d_attn, d_pool, prompt, layer):
    q_ref, k_ref, v_ref, kb_ref, vb_ref, ga_ref, u_ref, gb_ref = rest[-8:]
    x = x_ref[0]
    ms = jnp.mean(x * x, axis=-1, keepdims=True)
    h = (x * lax.rsqrt(ms + RMS_EPS) * g_ref[...]).astype(BF16)

    def proj(lo, width):
        return jnp.dot(h, w_ref[:, lo:lo + width], preferred_element_type=F32)

    q_ref[0] = (proj(0, d_attn) * -SB_SCALE).astype(BF16)
    kvt = lax.dot_general(wkvt_ref[...], h, _NT, preferred_element_type=F32)
    kb_ref[0] = kvt[:d_attn].astype(BF16)
    vb_ref[0] = kvt[d_attn:].astype(BF16)
    if prompt:
        if len(k_ref.shape) == 4:
            for l in range(k_ref.shape[0]):
                k_ref[l, 0] = kvt[:d_attn] if l == layer else jnp.zeros_like(kvt[:d_attn])
                v_ref[l, 0] = kvt[d_attn:] if l == layer else jnp.zeros_like(kvt[d_attn:])
        else:
            k_ref[0] = kvt[:d_attn]
            v_ref[0] = kvt[d_attn:]
    else:
        k_ref[0] = proj(d_attn, d_attn)
        v_ref[0] = proj(2 * d_attn, d_attn)
    ga_ref[0] = proj(3 * d_attn, d_attn).astype(ga_ref.dtype)
    u_ref[0] = proj(4 * d_attn, d_pool)
    gb_ref[0] = proj(4 * d_attn + d_pool, d_pool).astype(gb_ref.dtype)


def _norm_proj(x, g, w_bf16, wkvt_bf16, d_attn, d_pool, tm, prompt, layer=0, depth=1, kv_prev=None):
    b, t, d = x.shape
    d_in = w_bf16.shape[1]
    tok = lambda width: pl.BlockSpec((1, tm, width), lambda bi, i: (bi, i, 0))
    feat = pl.BlockSpec((1, d_attn, tm), lambda bi, i: (bi, 0, i))
    tok_shape = lambda width, dt: jax.ShapeDtypeStruct((b, t, width), dt)
    feat_shape = lambda dt: jax.ShapeDtypeStruct((b, d_attn, t), dt)
    if prompt:
        if kv_prev is None:
            layered = pl.BlockSpec((depth, 1, d_attn, tm), lambda bi, i: (0, bi, 0, i))
        else:
            layered = pl.BlockSpec((None, 1, d_attn, tm), lambda bi, i: (layer, bi, 0, i))
        layered_shape = jax.ShapeDtypeStruct((depth, b, d_attn, t), F32)
        kv_specs = [layered, layered, feat, feat]
        kv_shapes = [layered_shape, layered_shape, feat_shape(BF16), feat_shape(BF16)]
    else:
        kv_specs = [tok(d_attn), tok(d_attn), feat, feat]
        kv_shapes = [tok_shape(d_attn, F32), tok_shape(d_attn, F32), feat_shape(BF16), feat_shape(BF16)]
    gate_dtype = BF16 if prompt else F32
    prev = list(kv_prev) if kv_prev is not None else []
    n_in = 4
    return pl.pallas_call(
        functools.partial(_norm_proj_kernel, d_attn=d_attn, d_pool=d_pool, prompt=prompt,
                          layer=layer),
        grid=(b, t // tm),
        in_specs=[
            tok(d),
            pl.BlockSpec((1, d), lambda bi, i: (0, 0)),
            pl.BlockSpec((d, d_in), lambda bi, i: (0, 0)),
            pl.BlockSpec((2 * d_attn, d), lambda bi, i: (0, 0)),
        ] + [pl.BlockSpec(memory_space=pl.ANY)] * len(prev),
        out_specs=[tok(d_attn)] + kv_specs + [tok(d_attn), tok(d_pool), tok(d_pool)],
        out_shape=[tok_shape(d_attn, BF16)] + kv_shapes
        + [tok_shape(d_attn, gate_dtype), tok_shape(d_pool, F32), tok_shape(d_pool, gate_dtype)],
        input_output_aliases={n_in + n: 1 + n for n in range(len(prev))},
        compiler_params=_params("parallel", "parallel"),
        name="norm_proj",
    )(x, g.reshape(1, d), w_bf16, wkvt_bf16, *prev)


def _log_keep(nz, mask):
    e = jnp.exp2(jnp.abs(nz) * -LOG2_E)
    log_keep = jnp.minimum(nz, 0.0) - jnp.log(1.0 + e)
    if mask is not None:
        log_keep = jnp.where(mask, log_keep, 0.0)
    return log_keep


def _suffix_matrix(n):
    j = lax.broadcasted_iota(jnp.int32, (n, n), 0)
    s = lax.broadcasted_iota(jnp.int32, (n, n), 1)
    return jnp.where(j >= s, 1.0, 0.0).astype(BF16)


def _suffix_sums(x, suffix):
    return jnp.dot(x.astype(BF16), suffix, preferred_element_type=F32)


def _sb_weights(nz, incl, c, mask):
    w = jnp.exp(incl + c - nz)
    if mask is not None:
        w = jnp.where(mask, w, 0.0)
    return w.astype(BF16)


SAMPLE_PAGES_PER_VISIT = 16


def _prompt_part(bias_ref, q_ref, kt_ref, vt_ref, o_ref, c_ref, acc_ref, *, hp, i, blk,
                 heads_per_step, between):
    suffix = _suffix_matrix(blk)
    row = lax.broadcasted_iota(jnp.int32, (blk, blk), 0)
    col = lax.broadcasted_iota(jnp.int32, (blk, blk), 1)
    diag_mask = col < row
    feats = [slice(hh * HEAD_DIM, (hh + 1) * HEAD_DIM) for hh in range(heads_per_step)]
    heads = range(heads_per_step)
    extra = lax.broadcasted_iota(jnp.int32, (blk, HEAD_DIM), 1)
    ones = jnp.where(extra < 3, 1.0, 0.0).astype(BF16)
    qn = [jnp.concatenate([q_ref[0, :, f], ones], axis=1) for f in feats]
    term_row = lax.broadcasted_iota(jnp.int32, (HEAD_DIM, blk), 0)
    bias_rows = []
    for hh in heads:
        rest = jnp.full((HEAD_DIM, blk), -bias_ref[hp * heads_per_step + hh], F32)
        rows = jnp.zeros((HEAD_DIM, blk), F32)
        for term in range(3):
            piece = rest.astype(BF16).astype(F32)
            rows = jnp.where(term_row == term, piece, rows)
            rest = rest - piece
        bias_rows.append(rows.astype(BF16))

    def block(kb, mask):
        keys = pl.ds(pl.multiple_of(kb * blk, blk), blk)
        nz = [jnp.dot(qn[h], jnp.concatenate([kt_ref[0, feats[h], keys], bias_rows[h]], axis=0),
                      preferred_element_type=F32) for h in heads]
        log_keep = [_log_keep(nz[h], mask) for h in heads]
        incl = [_suffix_sums(log_keep[h], suffix) for h in heads]
        for h in heads:
            c = c_ref[h]
            w = _sb_weights(nz[h], incl[h], jnp.concatenate([c] * (blk // 128), axis=1), mask)
            acc_ref[h] += lax.dot_general(w, vt_ref[0, feats[h], keys], _NT,
                                          preferred_element_type=F32)
            c_ref[h] = c + jnp.broadcast_to(incl[h][:, :1], c.shape)

    c_ref[...] = jnp.zeros_like(c_ref)
    acc_ref[...] = jnp.zeros_like(acc_ref)
    block(i, diag_mask)
    between()

    def body(n, _):
        block(i - 1 - n, None)
        return 0

    lax.fori_loop(0, i, body, 0)
    o_ref[0] = jnp.concatenate([acc_ref[h] for h in heads], axis=-1).astype(o_ref.dtype)


def _sample_init(bias_ref, q_ref, qbd_ref, bias_scr, c_ref, acc_ref, *, n_heads, t_new):
    rows_all, d_attn = qbd_ref.shape
    r = lax.broadcasted_iota(jnp.int32, (rows_all, d_attn), 0)
    lane = lax.broadcasted_iota(jnp.int32, (rows_all, d_attn), 1)
    q_rep = jnp.concatenate([q_ref[0].astype(F32)] * n_heads, axis=0)
    qbd_ref[...] = jnp.where(r // t_new == lane // HEAD_DIM, q_rep, 0.0).astype(BF16)
    rb = lax.broadcasted_iota(jnp.int32, (rows_all, 1), 0) // t_new
    bias_col = jnp.zeros((rows_all, 1), F32)
    for h in range(n_heads):
        bias_col = jnp.where(rb == h, bias_ref[h], bias_col)
    bias_scr[...] = bias_col
    c_ref[...] = jnp.zeros_like(c_ref)
    acc_ref[...] = jnp.zeros_like(acc_ref)


def _sample_visit(kts, vts, mask, qbd_ref, bias_scr, c_ref, acc_ref):
    rows_all = qbd_ref.shape[0]
    suffix = _suffix_matrix(PAGE_SIZE)
    cols = [slice(p * PAGE_SIZE, (p + 1) * PAGE_SIZE) for p in range(len(kts))]
    nz = jnp.dot(qbd_ref[...], jnp.concatenate(kts, axis=1),
                 preferred_element_type=F32) - bias_scr[...]
    log_keep = _log_keep(nz, mask)
    incl = _suffix_sums(jnp.concatenate([log_keep[:, s] for s in cols], axis=0), suffix)
    c = c_ref[...]
    ws = []
    for p, s in enumerate(cols):
        incl_p = incl[p * rows_all:(p + 1) * rows_all]
        ws.append(_sb_weights(nz[:, s], incl_p, c, mask))
        c = c + incl_p[:, :1]
    c_ref[...] = c
    acc_ref[...] += lax.dot_general(jnp.concatenate(ws, axis=1), jnp.concatenate(vts, axis=1),
                                    _NT, preferred_element_type=F32)


def _page_copies(pt_ref, cache_refs, page_bufs, sem, *, layer, step, slot, n_pages,
                 pages_per_step, chunks_per_seq, lookup):
    seq = step // chunks_per_seq
    newest = n_pages - 1 - (step % chunks_per_seq) * pages_per_step
    copies = []
    for p in range(pages_per_step):
        page = pt_ref[seq * n_pages + newest - p] if lookup else 0
        for which, (cache, buf) in enumerate(zip(cache_refs, page_bufs)):
            copies.append(pltpu.make_async_copy(cache.at[layer, page], buf.at[slot, p],
                                                sem.at[which, slot]))
    return copies


def _attn_kernel(pt_ref, bias_ref, q_ref, kt_ref, vt_ref, sq_ref, knt_ref, vnt_ref, ktc_ref,
                 vtc_ref, o_ref, so_ref, c_ref, acc_ref, qbd_ref, bias_scr, sc_ref, sacc_ref,
                 kbuf, vbuf, sem, *, blk, heads_per_step, pages_per_step, chunks_per_seq,
                 n_heads, t_new, layer, n_pages):
    hp = pl.program_id(1)
    i = pl.program_id(2)
    n_steps = pl.num_programs(0) * pl.num_programs(1) * pl.num_programs(2)
    step = (pl.program_id(0) * pl.num_programs(1) + hp) * pl.num_programs(2) + i
    chunk = step % chunks_per_seq
    d_attn = n_heads * HEAD_DIM
    sample_state = (qbd_ref, bias_scr, sc_ref, sacc_ref)

    slot = step % 2
    copies = functools.partial(
        _page_copies, pt_ref, (ktc_ref, vtc_ref), (kbuf, vbuf), sem, layer=layer,
        n_pages=n_pages, pages_per_step=pages_per_step, chunks_per_seq=chunks_per_seq)

    @pl.when(step == 0)
    def _():
        for cp in copies(step=step, slot=slot, lookup=True):
            cp.start()

    for cp in copies(step=step, slot=slot, lookup=False):
        cp.wait()

    @pl.when(step + 1 < n_steps)
    def _():
        for cp in copies(step=step + 1, slot=1 - slot, lookup=True):
            cp.start()

    @pl.when(chunk == 0)
    def _():
        _sample_init(bias_ref, sq_ref, *sample_state, n_heads=n_heads, t_new=t_new)
        rows_all = n_heads * t_new
        t_row = lax.broadcasted_iota(jnp.int32, (rows_all, PAGE_SIZE), 0) % t_new
        s_col = lax.broadcasted_iota(jnp.int32, (rows_all, PAGE_SIZE), 1)
        _sample_visit([knt_ref[0]], [vnt_ref[0]], s_col < t_row, *sample_state)

    def past_pages():
        for lo in range(0, pages_per_step, SAMPLE_PAGES_PER_VISIT):
            pages = range(lo, lo + SAMPLE_PAGES_PER_VISIT)
            _sample_visit([kbuf[slot, p].astype(BF16) for p in pages],
                          [vbuf[slot, p].astype(BF16) for p in pages], None, *sample_state)

    _prompt_part(bias_ref, q_ref, kt_ref, vt_ref, o_ref, c_ref, acc_ref, hp=hp, i=i, blk=blk,
                 heads_per_step=heads_per_step, between=past_pages)

    @pl.when(chunk == chunks_per_seq - 1)
    def _():
        acc = sacc_ref[...]
        lane = lax.broadcasted_iota(jnp.int32, (t_new, d_attn), 1) // HEAD_DIM
        out = jnp.zeros((t_new, d_attn), F32)
        for h in range(n_heads):
            out = jnp.where(lane == h, acc[h * t_new:(h + 1) * t_new, :], out)
        so_ref[0] = out


def _attention(q, kt, vt, sq, knt, vnt, bias, kt_cache, vt_cache, layer, page_table, blk=256,
               heads_per_step=4):
    b, t, d_attn = q.shape
    bs, t_new, _ = sq.shape
    n_heads = d_attn // HEAD_DIM
    n_pages = page_table.shape[1]
    width = heads_per_step * HEAD_DIM
    grid = (b, d_attn // width, t // blk)
    n_steps = grid[0] * grid[1] * grid[2]
    chunks_per_seq, rem = divmod(n_steps, bs)
    pages_per_step, rem2 = divmod(n_pages, chunks_per_seq)
    assert rem == 0 and rem2 == 0 and pages_per_step % SAMPLE_PAGES_PER_VISIT == 0
    rows_all = n_heads * t_new

    def step_of(bi, hp, i):
        return (bi * grid[1] + hp) * grid[2] + i

    per_seq = lambda rows, cols: pl.BlockSpec(
        (1, rows, cols), lambda bi, hp, i, pt: (step_of(bi, hp, i) // chunks_per_seq, 0, 0))
    page_buf = pltpu.VMEM((2, pages_per_step, d_attn, PAGE_SIZE), F32)
    return pl.pallas_call(
        functools.partial(_attn_kernel, blk=blk, heads_per_step=heads_per_step,
                          pages_per_step=pages_per_step, chunks_per_seq=chunks_per_seq,
                          n_heads=n_heads, t_new=t_new, layer=layer, n_pages=n_pages),
        grid_spec=pltpu.PrefetchScalarGridSpec(
            num_scalar_prefetch=1,
            grid=grid,
            in_specs=[
                pl.BlockSpec(memory_space=pltpu.SMEM),
                pl.BlockSpec((1, blk, width), lambda bi, hp, i, pt: (bi, i, hp)),
                pl.BlockSpec((1, width, t), lambda bi, hp, i, pt: (bi, hp, 0)),
                pl.BlockSpec((1, width, t), lambda bi, hp, i, pt: (bi, hp, 0)),
                per_seq(t_new, d_attn), per_seq(d_attn, PAGE_SIZE), per_seq(d_attn, PAGE_SIZE),
                pl.BlockSpec(memory_space=pl.ANY), pl.BlockSpec(memory_space=pl.ANY),
            ],
            out_specs=[pl.BlockSpec((1, blk, width), lambda bi, hp, i, pt: (bi, i, hp)),
                       per_seq(t_new, d_attn)],
            scratch_shapes=[
                pltpu.VMEM((heads_per_step, blk, 128), F32),
                pltpu.VMEM((heads_per_step, blk, HEAD_DIM), F32),
                pltpu.VMEM((rows_all, d_attn), BF16),
                pltpu.VMEM((rows_all, 1), F32),
                pltpu.VMEM((rows_all, 1), F32),
                pltpu.VMEM((rows_all, d_attn), F32),
                page_buf, page_buf,
                pltpu.SemaphoreType.DMA((2, 2)),
            ],
        ),
        out_shape=[jax.ShapeDtypeStruct((b, t, d_attn), BF16),
                   jax.ShapeDtypeStruct((bs, t_new, d_attn), F32)],
        compiler_params=_params("arbitrary", "arbitrary", "arbitrary"),
        name="sb_attn",
    )(page_table.reshape(-1), bias, q, kt, vt, sq, knt, vnt, kt_cache, vt_cache)


def _silu(x):
    return x * (1.0 / (1.0 + jnp.exp(-x)))


def _mix_out_kernel(x_ref, a_ref, ga_ref, u_ref, uh_ref, pre_ref, gb_ref, wp_ref, ps_ref,
                    wo_ref, fg_ref, o_ref, seq_ref, *, tt, pos0, final):
    nb = x_ref.shape[0]
    rows = nb * tt
    j = pl.program_id(1)
    seq_ref[:, 0:HALO, :] = jnp.where(j == 0, pre_ref[...], uh_ref[...])
    seq_ref[:, HALO:, :] = u_ref[...]
    pos = pos0 + j * tt + lax.broadcasted_iota(jnp.int32, (tt, 1), 0)
    flat = lambda ref: ref[...].reshape(rows, ref.shape[-1]).astype(F32)
    gb = flat(gb_ref)
    parts = [flat(a_ref) * _silu(flat(ga_ref))]
    for g, w in enumerate(POOL_WINDOWS):
        lanes = slice(g * POOL_GROUP_DIM, (g + 1) * POOL_GROUP_DIM)
        count = jnp.minimum(pos + 1, w).astype(F32)
        diffs = []
        for s in range(nb):
            win_sum = seq_ref[s, HALO:HALO + tt, lanes]
            for back in range(1, w):
                win_sum = win_sum + seq_ref[s, HALO - back:HALO - back + tt, lanes]
            diffs.append(win_sum / count - seq_ref[s, HALO:HALO + tt, lanes])
        diff = jnp.concatenate(diffs, axis=0)
        y = jnp.dot(diff.astype(BF16), wp_ref[g], preferred_element_type=F32) * ps_ref[:, lanes]
        parts.append(y * _silu(gb[:, lanes]))
    mix = jnp.concatenate(parts, axis=-1).astype(BF16)
    h = flat(x_ref) + jnp.dot(mix, wo_ref[...], preferred_element_type=F32)
    if final:
        ms = jnp.mean(h * h, axis=-1, keepdims=True)
        h = h * lax.rsqrt(ms + RMS_EPS) * fg_ref[...]
    o_ref[...] = h.reshape(o_ref.shape)


def _mix_out(x, a, gate_a, u, prefix, gate_b, w_pool, pool_scale, w_out, final_g, *, nb, tt, pos0,
             final):
    b, t, d = x.shape
    d_attn = a.shape[-1]
    d_pool = u.shape[-1]
    tile = lambda width: pl.BlockSpec((nb, tt, width), lambda bi, j: (bi, j, 0))
    const = lambda shape: pl.BlockSpec(shape, lambda bi, j: (0,) * len(shape))
    halo_blocks = tt // HALO if tt >= HALO else 1
    halo_spec = pl.BlockSpec(
        (nb, HALO, d_pool), lambda bi, j: (bi, jnp.maximum(j * halo_blocks - 1, 0), 0))
    u_halo = u if t >= HALO else prefix
    return pl.pallas_call(
        functools.partial(_mix_out_kernel, tt=tt, pos0=pos0, final=final),
        grid=(b // nb, t // tt),
        in_specs=[
            tile(d), tile(d_attn), tile(d_attn), tile(d_pool), halo_spec,
            pl.BlockSpec((nb, HALO, d_pool), lambda bi, j: (bi, 0, 0)),
            tile(d_pool),
            const(w_pool.shape), const((1, d_pool)), const(w_out.shape), const((1, d)),
        ],
        out_specs=tile(d),
        out_shape=jax.ShapeDtypeStruct((b, t, d), F32),
        scratch_shapes=[pltpu.VMEM((nb, HALO + tt, d_pool), F32)],
        compiler_params=_params("parallel", "arbitrary"),
        name="mix_out",
    )(x, a, gate_a, u, u_halo, prefix, gate_b, w_pool, pool_scale.reshape(1, d_pool), w_out,
      final_g.reshape(1, d))


def kernel(x_prompt, x_sample, cache_k, cache_v, state_pool, page_table, norm_g, w_in, sb_bias,
           w_pool, pool_scale, w_out, final_g):
    depth = norm_g.shape[0]
    bp, tp, d_model = x_prompt.shape
    bs, ts, _ = x_sample.shape
    n_heads = cache_k.shape[3]
    d_attn = n_heads * HEAD_DIM
    d_pool = state_pool.shape[-1]
    n_pool = cache_k.shape[1]
    past = page_table.shape[1] * PAGE_SIZE

    w_in_b = w_in.astype(BF16)
    wkvt_b = jnp.swapaxes(w_in_b[:, :, d_attn:3 * d_attn], 1, 2)
    w_pool_b = w_pool.astype(BF16)
    w_out_b = w_out.astype(BF16)
    kt_cache = cache_k.transpose(0, 1, 3, 4, 2).reshape(depth, n_pool, d_attn, PAGE_SIZE)
    vt_cache = cache_v.transpose(0, 1, 3, 4, 2).reshape(depth, n_pool, d_attn, PAGE_SIZE)

    h_p = x_prompt
    h_s = x_sample.reshape(1, bs * ts, d_model)
    zero_prefix = jnp.zeros((bp, HALO, d_pool), F32)
    per_seq = lambda z: z.reshape(bs, ts, -1)
    kv_prompt = None
    pool_prompt, k_sample, v_sample, pool_sample = [], [], [], []
    for l in range(depth):
        final = l == depth - 1
        q, kt, vt, ktb, vtb, ga, u, gb = _norm_proj(
            h_p, norm_g[l], w_in_b[l], wkvt_b[l], d_attn, d_pool, tm=512, prompt=True,
            layer=l, depth=depth, kv_prev=kv_prompt)
        kv_prompt = (kt, vt)
        sq, kf, vf, sktb, svtb, sga, su, sgb = _norm_proj(
            h_s, norm_g[l], w_in_b[l], wkvt_b[l], d_attn, d_pool, tm=bs * ts, prompt=False)

        def new_page(zt):
            zt = zt.reshape(d_attn, bs, ts).transpose(1, 0, 2)
            return jnp.pad(zt, ((0, 0), (0, 0), (0, PAGE_SIZE - ts)))

        a_p, a_s = _attention(q, ktb, vtb, per_seq(sq), new_page(sktb), new_page(svtb), sb_bias[l],
                              kt_cache, vt_cache, l, page_table)
        h_p = _mix_out(h_p, a_p, ga, u, zero_prefix, gb, w_pool_b[l], pool_scale[l], w_out_b[l],
                       final_g, nb=1, tt=512, pos0=0, final=final)
        prefix = jnp.pad(state_pool[l], ((0, 0), (HALO - POOL_STATE_LEN, 0), (0, 0)))
        h_s = _mix_out(per_seq(h_s), a_s, per_seq(sga), per_seq(su), prefix, per_seq(sgb),
                       w_pool_b[l], pool_scale[l], w_out_b[l], final_g, nb=bs, tt=ts, pos0=past,
                       final=final).reshape(1, bs * ts, d_model)
        pool_prompt.append(u[:, tp - POOL_STATE_LEN:])
        k_sample.append(kf.reshape(bs, ts, n_heads, HEAD_DIM))
        v_sample.append(vf.reshape(bs, ts, n_heads, HEAD_DIM))
        pool_sample.append(
            jnp.concatenate([state_pool[l], per_seq(su)], axis=1)[:, -POOL_STATE_LEN:])

    def token_major(zt):
        return zt.reshape(depth, bp, n_heads, HEAD_DIM, tp).transpose(0, 1, 4, 2, 3)

    return (h_p, h_s.reshape(bs, ts, d_model), token_major(kv_prompt[0]), token_major(kv_prompt[1]),
            jnp.stack(pool_prompt), jnp.stack(k_sample), jnp.stack(v_sample), jnp.stack(pool_sample))
```

```python
import functools
import math

import jax
import jax.numpy as jnp
from jax import lax
from jax.experimental import pallas as pl
from jax.experimental.pallas import tpu as pltpu

F32 = jnp.float32
BF16 = jnp.bfloat16

PAGE_SIZE = 128
HEAD_DIM = 64
POOL_WINDOWS = (2, 4, 8, 16)
POOL_GROUP_DIM = 128
POOL_STATE_LEN = max(POOL_WINDOWS) - 1
HALO = POOL_STATE_LEN + 1
RMS_EPS = 1e-6
SB_SCALE = 1.0 / math.sqrt(HEAD_DIM)
LOG2_E = math.log2(math.e)

VMEM_LIMIT_BYTES = 56 * 1024 * 1024

ROW_TILE = 512
ATTN_BLOCK = 256
ATTN_HEADS_PER_STEP = 4
SAMPLE_PAGES_PER_VISIT = 32

_NT = (((1,), (1,)), ((), ()))


def _params(*sem):
    return pltpu.CompilerParams(dimension_semantics=sem, vmem_limit_bytes=VMEM_LIMIT_BYTES)


def _norm_proj_kernel(x_ref, g_ref, w_ref, wkvt_ref, *rest, d_attn, d_pool, prompt, layer):
    q_ref, k_ref, v_ref, kb_ref, vb_ref, ga_ref, u_ref, gb_ref = rest[-8:]
    x = x_ref[0]
    ms = jnp.mean(x * x, axis=-1, keepdims=True)
    h = (x * lax.rsqrt(ms + RMS_EPS) * g_ref[...]).astype(BF16)

    def proj(lo, width):
        return jnp.dot(h, w_ref[:, lo:lo + width], preferred_element_type=F32)

    q_ref[0] = (proj(0, d_attn) * -SB_SCALE).astype(BF16)
    kvt = lax.dot_general(wkvt_ref[...], h, _NT, preferred_element_type=F32)
    kb_ref[0] = kvt[:d_attn].astype(BF16)
    vb_ref[0] = kvt[d_attn:].astype(BF16)
    if prompt:
        if len(k_ref.shape) == 4:
            for l in range(k_ref.shape[0]):
                k_ref[l, 0] = kvt[:d_attn] if l == layer else jnp.zeros_like(kvt[:d_attn])
                v_ref[l, 0] = kvt[d_attn:] if l == layer else jnp.zeros_like(kvt[d_attn:])
        else:
            k_ref[0] = kvt[:d_attn]
            v_ref[0] = kvt[d_attn:]
    else:
        k_ref[0] = proj(d_attn, d_attn)
        v_ref[0] = proj(2 * d_attn, d_attn)
    ga_ref[0] = proj(3 * d_attn, d_attn).astype(ga_ref.dtype)
    u_ref[0] = proj(4 * d_attn, d_pool)
    gb_ref[0] = proj(4 * d_attn + d_pool, d_pool).astype(gb_ref.dtype)


def _norm_proj(x, g, w_bf16, wkvt_bf16, d_attn, d_pool, tm, prompt, layer=0, depth=1, kv_prev=None):
    b, t, d = x.shape
    d_in = w_bf16.shape[1]
    tok = lambda width: pl.BlockSpec((1, tm, width), lambda bi, i: (bi, i, 0))
    feat = pl.BlockSpec((1, d_attn, tm), lambda bi, i: (bi, 0, i))
    tok_shape = lambda width, dt: jax.ShapeDtypeStruct((b, t, width), dt)
    feat_shape = lambda dt: jax.ShapeDtypeStruct((b, d_attn, t), dt)
    if prompt:
        if kv_prev is None:
            layered = pl.BlockSpec((depth, 1, d_attn, tm), lambda bi, i: (0, bi, 0, i))
        else:
            layered = pl.BlockSpec((None, 1, d_attn, tm), lambda bi, i: (layer, bi, 0, i))
        layered_shape = jax.ShapeDtypeStruct((depth, b, d_attn, t), F32)
        kv_specs = [layered, layered, feat, feat]
        kv_shapes = [layered_shape, layered_shape, feat_shape(BF16), feat_shape(BF16)]
    else:
        kv_specs = [tok(d_attn), tok(d_attn), feat, feat]
        kv_shapes = [tok_shape(d_attn, F32), tok_shape(d_attn, F32), feat_shape(BF16), feat_shape(BF16)]
    gate_dtype = BF16 if prompt else F32
    prev = list(kv_prev) if kv_prev is not None else []
    n_in = 4
    return pl.pallas_call(
        functools.partial(_norm_proj_kernel, d_attn=d_attn, d_pool=d_pool, prompt=prompt,
                          layer=layer),
        grid=(b, t // tm),
        in_specs=[
            tok(d),
            pl.BlockSpec((1, d), lambda bi, i: (0, 0)),
            pl.BlockSpec((d, d_in), lambda bi, i: (0, 0)),
            pl.BlockSpec((2 * d_attn, d), lambda bi, i: (0, 0)),
        ] + [pl.BlockSpec(memory_space=pl.ANY)] * len(prev),
        out_specs=[tok(d_attn)] + kv_specs + [tok(d_attn), tok(d_pool), tok(d_pool)],
        out_shape=[tok_shape(d_attn, BF16)] + kv_shapes
        + [tok_shape(d_attn, gate_dtype), tok_shape(d_pool, F32), tok_shape(d_pool, gate_dtype)],
        input_output_aliases={n_in + n: 1 + n for n in range(len(prev))},
        compiler_params=_params("parallel", "parallel"),
        name="norm_proj",
    )(x, g.reshape(1, d), w_bf16, wkvt_bf16, *prev)


def _log_keep(nz, mask):
    e = jnp.exp2(jnp.abs(nz) * -LOG2_E)
    log_keep = jnp.minimum(nz, 0.0) - jnp.log(1.0 + e)
    if mask is not None:
        log_keep = jnp.where(mask, log_keep, 0.0)
    return log_keep


def _suffix_matrix(n):
    j = lax.broadcasted_iota(jnp.int32, (n, n), 0)
    s = lax.broadcasted_iota(jnp.int32, (n, n), 1)
    return jnp.where(j >= s, 1.0, 0.0).astype(BF16)


def _suffix_sums(x, suffix):
    return jnp.dot(x.astype(BF16), suffix, preferred_element_type=F32)


def _sb_weights(nz, incl, c, mask):
    w = jnp.exp(incl + c - nz)
    if mask is not None:
        w = jnp.where(mask, w, 0.0)
    return w.astype(BF16)


def _prompt_part(bias_ref, q_ref, kt_ref, vt_ref, o_ref, c_ref, acc_ref, *, hp, i, blk,
                 heads_per_step, between):
    suffix = _suffix_matrix(blk)
    row = lax.broadcasted_iota(jnp.int32, (blk, blk), 0)
    col = lax.broadcasted_iota(jnp.int32, (blk, blk), 1)
    diag_mask = col < row
    feats = [slice(hh * HEAD_DIM, (hh + 1) * HEAD_DIM) for hh in range(heads_per_step)]
    heads = range(heads_per_step)
    extra = lax.broadcasted_iota(jnp.int32, (blk, HEAD_DIM), 1)
    ones = jnp.where(extra < 3, 1.0, 0.0).astype(BF16)
    qn = [jnp.concatenate([q_ref[0, :, f], ones], axis=1) for f in feats]
    term_row = lax.broadcasted_iota(jnp.int32, (HEAD_DIM, blk), 0)
    bias_rows = []
    for hh in heads:
        rest = jnp.full((HEAD_DIM, blk), -bias_ref[hp * heads_per_step + hh], F32)
        rows = jnp.zeros((HEAD_DIM, blk), F32)
        for term in range(3):
            piece = rest.astype(BF16).astype(F32)
            rows = jnp.where(term_row == term, piece, rows)
            rest = rest - piece
        bias_rows.append(rows.astype(BF16))

    def block(kb, mask):
        keys = pl.ds(pl.multiple_of(kb * blk, blk), blk)
        nz = [jnp.dot(qn[h], jnp.concatenate([kt_ref[0, feats[h], keys], bias_rows[h]], axis=0),
                      preferred_element_type=F32) for h in heads]
        log_keep = [_log_keep(nz[h], mask) for h in heads]
        incl = [_suffix_sums(log_keep[h], suffix) for h in heads]
        for h in heads:
            c = c_ref[h]
            w = _sb_weights(nz[h], incl[h], jnp.concatenate([c] * (blk // 128), axis=1), mask)
            acc_ref[h] += lax.dot_general(w, vt_ref[0, feats[h], keys], _NT,
                                          preferred_element_type=F32)
            c_ref[h] = c + jnp.broadcast_to(incl[h][:, :1], c.shape)

    c_ref[...] = jnp.zeros_like(c_ref)
    acc_ref[...] = jnp.zeros_like(acc_ref)
    block(i, diag_mask)
    between()

    def body(n, _):
        block(i - 1 - n, None)
        return 0

    lax.fori_loop(0, i, body, 0)
    o_ref[0] = jnp.concatenate([acc_ref[h] for h in heads], axis=-1).astype(o_ref.dtype)


def _sample_init(bias_ref, q_ref, qbd_ref, bias_scr, c_ref, acc_ref, *, n_heads, t_new):
    rows_all, d_attn = qbd_ref.shape
    r = lax.broadcasted_iota(jnp.int32, (rows_all, d_attn), 0)
    lane = lax.broadcasted_iota(jnp.int32, (rows_all, d_attn), 1)
    q_rep = jnp.concatenate([q_ref[0].astype(F32)] * n_heads, axis=0)
    qbd_ref[...] = jnp.where(r // t_new == lane // HEAD_DIM, q_rep, 0.0).astype(BF16)
    rb = lax.broadcasted_iota(jnp.int32, (rows_all, 1), 0) // t_new
    bias_col = jnp.zeros((rows_all, 1), F32)
    for h in range(n_heads):
        bias_col = jnp.where(rb == h, bias_ref[h], bias_col)
    bias_scr[...] = bias_col
    c_ref[...] = jnp.zeros_like(c_ref)
    acc_ref[...] = jnp.zeros_like(acc_ref)


def _sample_visit(kts, vts, mask, qbd_ref, bias_scr, c_ref, acc_ref):
    rows_all = qbd_ref.shape[0]
    suffix = _suffix_matrix(PAGE_SIZE)
    cols = [slice(p * PAGE_SIZE, (p + 1) * PAGE_SIZE) for p in range(len(kts))]
    nz = jnp.dot(qbd_ref[...], jnp.concatenate(kts, axis=1),
                 preferred_element_type=F32) - bias_scr[...]
    log_keep = _log_keep(nz, mask)
    incl = _suffix_sums(jnp.concatenate([log_keep[:, s] for s in cols], axis=0), suffix)
    c = c_ref[...]
    ws = []
    for p, s in enumerate(cols):
        incl_p = incl[p * rows_all:(p + 1) * rows_all]
        ws.append(_sb_weights(nz[:, s], incl_p, c, mask))
        c = c + incl_p[:, :1]
    c_ref[...] = c
    acc_ref[...] += lax.dot_general(jnp.concatenate(ws, axis=1), jnp.concatenate(vts, axis=1),
                                    _NT, preferred_element_type=F32)


def _page_copies(pt_ref, cache_refs, page_bufs, sem, *, layer, step, slot, n_pages,
                 pages_per_step, chunks_per_seq, lookup):
    seq = step // chunks_per_seq
    newest = n_pages - 1 - (step % chunks_per_seq) * pages_per_step
    copies = []
    for p in range(pages_per_step):
        page = pt_ref[seq * n_pages + newest - p] if lookup else 0
        for which, (cache, buf) in enumerate(zip(cache_refs, page_bufs)):
            copies.append(pltpu.make_async_copy(cache.at[layer, page], buf.at[slot, p],
                                                sem.at[which, slot]))
    return copies


def _attn_kernel(pt_ref, bias_ref, q_ref, kt_ref, vt_ref, sq_ref, knt_ref, vnt_ref, ktc_ref,
                 vtc_ref, o_ref, so_ref, c_ref, acc_ref, qbd_ref, bias_scr, sc_ref, sacc_ref,
                 kbuf, vbuf, sem, *, blk, heads_per_step, pages_per_step, chunks_per_seq,
                 n_heads, t_new, layer, n_pages):
    hp = pl.program_id(1)
    i = pl.program_id(2)
    n_steps = pl.num_programs(0) * pl.num_programs(1) * pl.num_programs(2)
    step = (pl.program_id(0) * pl.num_programs(1) + hp) * pl.num_programs(2) + i
    chunk = step % chunks_per_seq
    d_attn = n_heads * HEAD_DIM
    sample_state = (qbd_ref, bias_scr, sc_ref, sacc_ref)

    slot = step % 2
    copies = functools.partial(
        _page_copies, pt_ref, (ktc_ref, vtc_ref), (kbuf, vbuf), sem, layer=layer,
        n_pages=n_pages, pages_per_step=pages_per_step, chunks_per_seq=chunks_per_seq)

    @pl.when(step == 0)
    def _():
        for cp in copies(step=step, slot=slot, lookup=True):
            cp.start()

    for cp in copies(step=step, slot=slot, lookup=False):
        cp.wait()

    @pl.when(step + 1 < n_steps)
    def _():
        for cp in copies(step=step + 1, slot=1 - slot, lookup=True):
            cp.start()

    @pl.when(chunk == 0)
    def _():
        _sample_init(bias_ref, sq_ref, *sample_state, n_heads=n_heads, t_new=t_new)
        rows_all = n_heads * t_new
        t_row = lax.broadcasted_iota(jnp.int32, (rows_all, PAGE_SIZE), 0) % t_new
        s_col = lax.broadcasted_iota(jnp.int32, (rows_all, PAGE_SIZE), 1)
        _sample_visit([knt_ref[0]], [vnt_ref[0]], s_col < t_row, *sample_state)

    def past_pages():
        for lo in range(0, pages_per_step, SAMPLE_PAGES_PER_VISIT):
            pages = range(lo, lo + SAMPLE_PAGES_PER_VISIT)
            _sample_visit([kbuf[slot, p].astype(BF16) for p in pages],
                          [vbuf[slot, p].astype(BF16) for p in pages], None, *sample_state)

    _prompt_part(bias_ref, q_ref, kt_ref, vt_ref, o_ref, c_ref, acc_ref, hp=hp, i=i, blk=blk,
                 heads_per_step=heads_per_step, between=past_pages)

    @pl.when(chunk == chunks_per_seq - 1)
    def _():
        acc = sacc_ref[...]
        lane = lax.broadcasted_iota(jnp.int32, (t_new, d_attn), 1) // HEAD_DIM
        out = jnp.zeros((t_new, d_attn), F32)
        for h in range(n_heads):
            out = jnp.where(lane == h, acc[h * t_new:(h + 1) * t_new, :], out)
        so_ref[0] = out


def _attention(q, kt, vt, sq, knt, vnt, bias, kt_cache, vt_cache, layer, page_table,
               blk=ATTN_BLOCK, heads_per_step=ATTN_HEADS_PER_STEP):
    b, t, d_attn = q.shape
    bs, t_new, _ = sq.shape
    n_heads = d_attn // HEAD_DIM
    n_pages = page_table.shape[1]
    width = heads_per_step * HEAD_DIM
    grid = (b, d_attn // width, t // blk)
    n_steps = grid[0] * grid[1] * grid[2]
    chunks_per_seq, rem = divmod(n_steps, bs)
    pages_per_step, rem2 = divmod(n_pages, chunks_per_seq)
    assert rem == 0 and rem2 == 0 and pages_per_step % SAMPLE_PAGES_PER_VISIT == 0
    rows_all = n_heads * t_new

    def step_of(bi, hp, i):
        return (bi * grid[1] + hp) * grid[2] + i

    per_seq = lambda rows, cols: pl.BlockSpec(
        (1, rows, cols), lambda bi, hp, i, pt: (step_of(bi, hp, i) // chunks_per_seq, 0, 0))
    page_buf = pltpu.VMEM((2, pages_per_step, d_attn, PAGE_SIZE), F32)
    return pl.pallas_call(
        functools.partial(_attn_kernel, blk=blk, heads_per_step=heads_per_step,
                          pages_per_step=pages_per_step, chunks_per_seq=chunks_per_seq,
                          n_heads=n_heads, t_new=t_new, layer=layer, n_pages=n_pages),
        grid_spec=pltpu.PrefetchScalarGridSpec(
            num_scalar_prefetch=1,
            grid=grid,
            in_specs=[
                pl.BlockSpec(memory_space=pltpu.SMEM),
                pl.BlockSpec((1, blk, width), lambda bi, hp, i, pt: (bi, i, hp)),
                pl.BlockSpec((1, width, t), lambda bi, hp, i, pt: (bi, hp, 0)),
                pl.BlockSpec((1, width, t), lambda bi, hp, i, pt: (bi, hp, 0)),
                per_seq(t_new, d_attn), per_seq(d_attn, PAGE_SIZE), per_seq(d_attn, PAGE_SIZE),
                pl.BlockSpec(memory_space=pl.ANY), pl.BlockSpec(memory_space=pl.ANY),
            ],
            out_specs=[pl.BlockSpec((1, blk, width), lambda bi, hp, i, pt: (bi, i, hp)),
                       per_seq(t_new, d_attn)],
            scratch_shapes=[
                pltpu.VMEM((heads_per_step, blk, 128), F32),
                pltpu.VMEM((heads_per_step, blk, HEAD_DIM), F32),
                pltpu.VMEM((rows_all, d_attn), BF16),
                pltpu.VMEM((rows_all, 1), F32),
                pltpu.VMEM((rows_all, 1), F32),
                pltpu.VMEM((rows_all, d_attn), F32),
                page_buf, page_buf,
                pltpu.SemaphoreType.DMA((2, 2)),
            ],
        ),
        out_shape=[jax.ShapeDtypeStruct((b, t, d_attn), BF16),
                   jax.ShapeDtypeStruct((bs, t_new, d_attn), F32)],
        compiler_params=_params("arbitrary", "arbitrary", "arbitrary"),
        name="sb_attn",
    )(page_table.reshape(-1), bias, q, kt, vt, sq, knt, vnt, kt_cache, vt_cache)


def _silu(x):
    return x * (1.0 / (1.0 + jnp.exp(-x)))


def _mix_out_kernel(x_ref, a_ref, ga_ref, u_ref, uh_ref, pre_ref, gb_ref, wp_ref, ps_ref,
                    wo_ref, fg_ref, o_ref, seq_ref, *, tt, pos0, final):
    nb = x_ref.shape[0]
    rows = nb * tt
    j = pl.program_id(1)
    seq_ref[:, 0:HALO, :] = jnp.where(j == 0, pre_ref[...], uh_ref[...])
    seq_ref[:, HALO:, :] = u_ref[...]
    pos = pos0 + j * tt + lax.broadcasted_iota(jnp.int32, (tt, 1), 0)
    flat = lambda ref: ref[...].reshape(rows, ref.shape[-1]).astype(F32)
    gb = flat(gb_ref)
    parts = [flat(a_ref) * _silu(flat(ga_ref))]
    for g, w in enumerate(POOL_WINDOWS):
        lanes = slice(g * POOL_GROUP_DIM, (g + 1) * POOL_GROUP_DIM)
        count = jnp.minimum(pos + 1, w).astype(F32)
        diffs = []
        for s in range(nb):
            win_sum = seq_ref[s, HALO:HALO + tt, lanes]
            for back in range(1, w):
                win_sum = win_sum + seq_ref[s, HALO - back:HALO - back + tt, lanes]
            diffs.append(win_sum / count - seq_ref[s, HALO:HALO + tt, lanes])
        diff = jnp.concatenate(diffs, axis=0)
        y = jnp.dot(diff.astype(BF16), wp_ref[g], preferred_element_type=F32) * ps_ref[:, lanes]
        parts.append(y * _silu(gb[:, lanes]))
    mix = jnp.concatenate(parts, axis=-1).astype(BF16)
    h = flat(x_ref) + jnp.dot(mix, wo_ref[...], preferred_element_type=F32)
    if final:
        ms = jnp.mean(h * h, axis=-1, keepdims=True)
        h = h * lax.rsqrt(ms + RMS_EPS) * fg_ref[...]
    o_ref[...] = h.reshape(o_ref.shape)


def _mix_out(x, a, gate_a, u, prefix, gate_b, w_pool, pool_scale, w_out, final_g, *, nb, tt, pos0,
             final):
    b, t, d = x.shape
    d_attn = a.shape[-1]
    d_pool = u.shape[-1]
    tile = lambda width: pl.BlockSpec((nb, tt, width), lambda bi, j: (bi, j, 0))
    const = lambda shape: pl.BlockSpec(shape, lambda bi, j: (0,) * len(shape))
    halo_blocks = tt // HALO if tt >= HALO else 1
    halo_spec = pl.BlockSpec(
        (nb, HALO, d_pool), lambda bi, j: (bi, jnp.maximum(j * halo_blocks - 1, 0), 0))
    u_halo = u if t >= HALO else prefix
    return pl.pallas_call(
        functools.partial(_mix_out_kernel, tt=tt, pos0=pos0, final=final),
        grid=(b // nb, t // tt),
        in_specs=[
            tile(d), tile(d_attn), tile(d_attn), tile(d_pool), halo_spec,
            pl.BlockSpec((nb, HALO, d_pool), lambda bi, j: (bi, 0, 0)),
            tile(d_pool),
            const(w_pool.shape), const((1, d_pool)), const(w_out.shape), const((1, d)),
        ],
        out_specs=tile(d),
        out_shape=jax.ShapeDtypeStruct((b, t, d), F32),
        scratch_shapes=[pltpu.VMEM((nb, HALO + tt, d_pool), F32)],
        compiler_params=_params("parallel", "arbitrary"),
        name="mix_out",
    )(x, a, gate_a, u, u_halo, prefix, gate_b, w_pool, pool_scale.reshape(1, d_pool), w_out,
      final_g.reshape(1, d))


def kernel(x_prompt, x_sample, cache_k, cache_v, state_pool, page_table, norm_g, w_in, sb_bias,
           w_pool, pool_scale, w_out, final_g):
    depth = norm_g.shape[0]
    bp, tp, d_model = x_prompt.shape
    bs, ts, _ = x_sample.shape
    n_heads = cache_k.shape[3]
    d_attn = n_heads * HEAD_DIM
    d_pool = state_pool.shape[-1]
    n_pool = cache_k.shape[1]
    past = page_table.shape[1] * PAGE_SIZE

    w_in_b = w_in.astype(BF16)
    wkvt_b = jnp.swapaxes(w_in_b[:, :, d_attn:3 * d_attn], 1, 2)
    w_pool_b = w_pool.astype(BF16)
    w_out_b = w_out.astype(BF16)
    kt_cache = cache_k.transpose(0, 1, 3, 4, 2).reshape(depth, n_pool, d_attn, PAGE_SIZE)
    vt_cache = cache_v.transpose(0, 1, 3, 4, 2).reshape(depth, n_pool, d_attn, PAGE_SIZE)

    h_p = x_prompt
    h_s = x_sample.reshape(1, bs * ts, d_model)
    zero_prefix = jnp.zeros((bp, HALO, d_pool), F32)
    per_seq = lambda z: z.reshape(bs, ts, -1)
    kv_prompt = None
    pool_prompt, k_sample, v_sample, pool_sample = [], [], [], []
    for l in range(depth):
        final = l == depth - 1
        q, kt, vt, ktb, vtb, ga, u, gb = _norm_proj(
            h_p, norm_g[l], w_in_b[l], wkvt_b[l], d_attn, d_pool, tm=ROW_TILE, prompt=True,
            layer=l, depth=depth, kv_prev=kv_prompt)
        kv_prompt = (kt, vt)
        sq, kf, vf, sktb, svtb, sga, su, sgb = _norm_proj(
            h_s, norm_g[l], w_in_b[l], wkvt_b[l], d_attn, d_pool, tm=bs * ts, prompt=False)

        def new_page(zt):
            zt = zt.reshape(d_attn, bs, ts).transpose(1, 0, 2)
            return jnp.pad(zt, ((0, 0), (0, 0), (0, PAGE_SIZE - ts)))

        a_p, a_s = _attention(q, ktb, vtb, per_seq(sq), new_page(sktb), new_page(svtb), sb_bias[l],
                              kt_cache, vt_cache, l, page_table)
        h_p = _mix_out(h_p, a_p, ga, u, zero_prefix, gb, w_pool_b[l], pool_scale[l], w_out_b[l],
                       final_g, nb=1, tt=ROW_TILE, pos0=0, final=final)
        prefix = jnp.pad(state_pool[l], ((0, 0), (HALO - POOL_STATE_LEN, 0), (0, 0)))
        h_s = _mix_out(per_seq(h_s), a_s, per_seq(sga), per_seq(su), prefix, per_seq(sgb),
                       w_pool_b[l], pool_scale[l], w_out_b[l], final_g, nb=bs, tt=ts, pos0=past,
                       final=final).reshape(1, bs * ts, d_model)
        pool_prompt.append(u[:, tp - POOL_STATE_LEN:])
        k_sample.append(kf.reshape(bs, ts, n_heads, HEAD_DIM))
        v_sample.append(vf.reshape(bs, ts, n_heads, HEAD_DIM))
        pool_sample.append(
            jnp.concatenate([state_pool[l], per_seq(su)], axis=1)[:, -POOL_STATE_LEN:])

    def token_major(zt):
        return zt.reshape(depth, bp, n_heads, HEAD_DIM, tp).transpose(0, 1, 4, 2, 3)

    return (h_p, h_s.reshape(bs, ts, d_model), token_major(kv_prompt[0]), token_major(kv_prompt[1]),
            jnp.stack(pool_prompt), jnp.stack(k_sample), jnp.stack(v_sample), jnp.stack(pool_sample))
```

```python
import functools
import math

import jax
import jax.numpy as jnp
from jax import lax
from jax.experimental import pallas as pl
from jax.experimental.pallas import tpu as pltpu

F32 = jnp.float32
BF16 = jnp.bfloat16

PAGE_SIZE = 128
HEAD_DIM = 64
POOL_WINDOWS = (2, 4, 8, 16)
POOL_GROUP_DIM = 128
POOL_STATE_LEN = max(POOL_WINDOWS) - 1
HALO = POOL_STATE_LEN + 1
RMS_EPS = 1e-6
SB_SCALE = 1.0 / math.sqrt(HEAD_DIM)
LOG2_E = math.log2(math.e)

VMEM_LIMIT_BYTES = 56 * 1024 * 1024

ROW_TILE = 512
ATTN_BLOCK = 256
ATTN_HEADS_PER_STEP = 4
SAMPLE_PAGES_PER_VISIT = 16

_NT = (((1,), (1,)), ((), ()))


def _params(*sem):
    return pltpu.CompilerParams(dimension_semantics=sem, vmem_limit_bytes=VMEM_LIMIT_BYTES)


def _norm_proj_kernel(x_ref, g_ref, w_ref, wkvt_ref, *rest, d_attn, d_pool, prompt, layer):
    q_ref, k_ref, v_ref, kb_ref, vb_ref, ga_ref, u_ref, gb_ref = rest[-8:]
    x = x_ref[0]
    ms = jnp.mean(x * x, axis=-1, keepdims=True)
    h = (x * lax.rsqrt(ms + RMS_EPS) * g_ref[...]).astype(BF16)

    def proj(lo, width):
        return jnp.dot(h, w_ref[:, lo:lo + width], preferred_element_type=F32)

    q_ref[0] = (proj(0, d_attn) * -SB_SCALE).astype(BF16)
    kvt = lax.dot_general(wkvt_ref[...], h, _NT, preferred_element_type=F32)
    kb_ref[0] = kvt[:d_attn].astype(BF16)
    vb_ref[0] = kvt[d_attn:].astype(BF16)
    if prompt:
        if len(k_ref.shape) == 4:
            for l in range(k_ref.shape[0]):
                k_ref[l, 0] = kvt[:d_attn] if l == layer else jnp.zeros_like(kvt[:d_attn])
                v_ref[l, 0] = kvt[d_attn:] if l == layer else jnp.zeros_like(kvt[d_attn:])
        else:
            k_ref[0] = kvt[:d_attn]
            v_ref[0] = kvt[d_attn:]
    else:
        k_ref[0] = proj(d_attn, d_attn)
        v_ref[0] = proj(2 * d_attn, d_attn)
    ga_ref[0] = proj(3 * d_attn, d_attn).astype(ga_ref.dtype)
    u_ref[0] = proj(4 * d_attn, d_pool)
    gb_ref[0] = proj(4 * d_attn + d_pool, d_pool).astype(gb_ref.dtype)


def _norm_proj(x, g, w_bf16, wkvt_bf16, d_attn, d_pool, tm, prompt, layer=0, depth=1, kv_prev=None):
    b, t, d = x.shape
    d_in = w_bf16.shape[1]
    tok = lambda width: pl.BlockSpec((1, tm, width), lambda bi, i: (bi, i, 0))
    feat = pl.BlockSpec((1, d_attn, tm), lambda bi, i: (bi, 0, i))
    tok_shape = lambda width, dt: jax.ShapeDtypeStruct((b, t, width), dt)
    feat_shape = lambda dt: jax.ShapeDtypeStruct((b, d_attn, t), dt)
    if prompt:
        if kv_prev is None:
            layered = pl.BlockSpec((depth, 1, d_attn, tm), lambda bi, i: (0, bi, 0, i))
        else:
            layered = pl.BlockSpec((None, 1, d_attn, tm), lambda bi, i: (layer, bi, 0, i))
        layered_shape = jax.ShapeDtypeStruct((depth, b, d_attn, t), F32)
        kv_specs = [layered, layered, feat, feat]
        kv_shapes = [layered_shape, layered_shape, feat_shape(BF16), feat_shape(BF16)]
    else:
        kv_specs = [tok(d_attn), tok(d_attn), feat, feat]
        kv_shapes = [tok_shape(d_attn, F32), tok_shape(d_attn, F32), feat_shape(BF16), feat_shape(BF16)]
    gate_dtype = BF16 if prompt else F32
    prev = list(kv_prev) if kv_prev is not None else []
    n_in = 4
    return pl.pallas_call(
        functools.partial(_norm_proj_kernel, d_attn=d_attn, d_pool=d_pool, prompt=prompt,
                          layer=layer),
        grid=(b, t // tm),
        in_specs=[
            tok(d),
            pl.BlockSpec((1, d), lambda bi, i: (0, 0)),
            pl.BlockSpec((d, d_in), lambda bi, i: (0, 0)),
            pl.BlockSpec((2 * d_attn, d), lambda bi, i: (0, 0)),
        ] + [pl.BlockSpec(memory_space=pl.ANY)] * len(prev),
        out_specs=[tok(d_attn)] + kv_specs + [tok(d_attn), tok(d_pool), tok(d_pool)],
        out_shape=[tok_shape(d_attn, BF16)] + kv_shapes
        + [tok_shape(d_attn, gate_dtype), tok_shape(d_pool, F32), tok_shape(d_pool, gate_dtype)],
        input_output_aliases={n_in + n: 1 + n for n in range(len(prev))},
        compiler_params=_params("parallel", "parallel"),
        name="norm_proj",
    )(x, g.reshape(1, d), w_bf16, wkvt_bf16, *prev)


def _log_keep(nz, mask):
    e = jnp.exp2(jnp.abs(nz) * -LOG2_E)
    log_keep = jnp.minimum(nz, 0.0) - jnp.log(1.0 + e)
    if mask is not None:
        log_keep = jnp.where(mask, log_keep, 0.0)
    return log_keep


def _suffix_matrix(n):
    j = lax.broadcasted_iota(jnp.int32, (n, n), 0)
    s = lax.broadcasted_iota(jnp.int32, (n, n), 1)
    return jnp.where(j >= s, 1.0, 0.0).astype(BF16)


def _suffix_sums(x, suffix):
    return jnp.dot(x.astype(BF16), suffix, preferred_element_type=F32)


def _sb_weights(nz, incl, c, mask):
    w = jnp.exp(incl + c - nz)
    if mask is not None:
        w = jnp.where(mask, w, 0.0)
    return w.astype(BF16)


def _prompt_part(bias_ref, q_ref, kt_ref, vt_ref, o_ref, c_ref, acc_ref, *, hp, i, blk,
                 heads_per_step, between):
    suffix = _suffix_matrix(blk)
    row = lax.broadcasted_iota(jnp.int32, (blk, blk), 0)
    col = lax.broadcasted_iota(jnp.int32, (blk, blk), 1)
    diag_mask = col < row
    feats = [slice(hh * HEAD_DIM, (hh + 1) * HEAD_DIM) for hh in range(heads_per_step)]
    heads = range(heads_per_step)
    extra = lax.broadcasted_iota(jnp.int32, (blk, HEAD_DIM), 1)
    ones = jnp.where(extra < 3, 1.0, 0.0).astype(BF16)
    qn = [jnp.concatenate([q_ref[0, :, f], ones], axis=1) for f in feats]
    term_row = lax.broadcasted_iota(jnp.int32, (HEAD_DIM, blk), 0)
    bias_rows = []
    for hh in heads:
        rest = jnp.full((HEAD_DIM, blk), -bias_ref[hp * heads_per_step + hh], F32)
        rows = jnp.zeros((HEAD_DIM, blk), F32)
        for term in range(3):
            piece = rest.astype(BF16).astype(F32)
            rows = jnp.where(term_row == term, piece, rows)
            rest = rest - piece
        bias_rows.append(rows.astype(BF16))

    def block(kb, mask):
        keys = pl.ds(pl.multiple_of(kb * blk, blk), blk)
        nz = [jnp.dot(qn[h], jnp.concatenate([kt_ref[0, feats[h], keys], bias_rows[h]], axis=0),
                      preferred_element_type=F32) for h in heads]
        log_keep = [_log_keep(nz[h], mask) for h in heads]
        incl = [_suffix_sums(log_keep[h], suffix) for h in heads]
        for h in heads:
            c = c_ref[h]
            w = _sb_weights(nz[h], incl[h], jnp.concatenate([c] * (blk // 128), axis=1), mask)
            acc_ref[h] += lax.dot_general(w, vt_ref[0, feats[h], keys], _NT,
                                          preferred_element_type=F32)
            c_ref[h] = c + jnp.broadcast_to(incl[h][:, :1], c.shape)

    c_ref[...] = jnp.zeros_like(c_ref)
    acc_ref[...] = jnp.zeros_like(acc_ref)
    block(i, diag_mask)
    between()

    def body(n, _):
        block(i - 1 - n, None)
        return 0

    lax.fori_loop(0, i, body, 0)
    o_ref[0] = jnp.concatenate([acc_ref[h] for h in heads], axis=-1).astype(o_ref.dtype)


def _sample_init(bias_ref, q_ref, qbd_ref, bias_scr, c_ref, acc_ref, *, n_heads, t_new):
    rows_all, d_attn = qbd_ref.shape
    r = lax.broadcasted_iota(jnp.int32, (rows_all, d_attn), 0)
    lane = lax.broadcasted_iota(jnp.int32, (rows_all, d_attn), 1)
    q_rep = jnp.concatenate([q_ref[0].astype(F32)] * n_heads, axis=0)
    qbd_ref[...] = jnp.where(r // t_new == lane // HEAD_DIM, q_rep, 0.0).astype(BF16)
    rb = lax.broadcasted_iota(jnp.int32, (rows_all, 1), 0) // t_new
    bias_col = jnp.zeros((rows_all, 1), F32)
    for h in range(n_heads):
        bias_col = jnp.where(rb == h, bias_ref[h], bias_col)
    bias_scr[...] = bias_col
    c_ref[...] = jnp.zeros_like(c_ref)
    acc_ref[...] = jnp.zeros_like(acc_ref)


def _sample_visit(kts, vts, mask, qbd_ref, bias_scr, c_ref, acc_ref):
    rows_all = qbd_ref.shape[0]
    suffix = _suffix_matrix(PAGE_SIZE)
    cols = [slice(p * PAGE_SIZE, (p + 1) * PAGE_SIZE) for p in range(len(kts))]
    nz = jnp.dot(qbd_ref[...], jnp.concatenate(kts, axis=1),
                 preferred_element_type=F32) - bias_scr[...]
    log_keep = _log_keep(nz, mask)
    incl = _suffix_sums(jnp.concatenate([log_keep[:, s] for s in cols], axis=0), suffix)
    c = c_ref[...]
    ws = []
    for p, s in enumerate(cols):
        incl_p = incl[p * rows_all:(p + 1) * rows_all]
        ws.append(_sb_weights(nz[:, s], incl_p, c, mask))
        c = c + incl_p[:, :1]
    c_ref[...] = c
    acc_ref[...] += lax.dot_general(jnp.concatenate(ws, axis=1), jnp.concatenate(vts, axis=1),
                                    _NT, preferred_element_type=F32)


def _page_copies(pt_ref, cache_refs, page_bufs, sem, *, layer, step, slot, n_pages,
                 pages_per_step, chunks_per_seq, lookup):
    seq = step // chunks_per_seq
    newest = n_pages - 1 - (step % chunks_per_seq) * pages_per_step
    copies = []
    for p in range(pages_per_step):
        page = pt_ref[seq * n_pages + newest - p] if lookup else 0
        for which, (cache, buf) in enumerate(zip(cache_refs, page_bufs)):
            copies.append(pltpu.make_async_copy(cache.at[layer, page], buf.at[slot, p],
                                                sem.at[which, slot]))
    return copies


def _attn_kernel(pt_ref, bias_ref, q_ref, kt_ref, vt_ref, sq_ref, knt_ref, vnt_ref, ktc_ref,
                 vtc_ref, o_ref, so_ref, c_ref, acc_ref, qbd_ref, bias_scr, sc_ref, sacc_ref,
                 kbuf, vbuf, sem, *, blk, heads_per_step, pages_per_step, chunks_per_seq,
                 n_heads, t_new, layer, n_pages):
    hp = pl.program_id(1)
    i = pl.program_id(2)
    n_steps = pl.num_programs(0) * pl.num_programs(1) * pl.num_programs(2)
    step = (pl.program_id(0) * pl.num_programs(1) + hp) * pl.num_programs(2) + i
    chunk = step % chunks_per_seq
    d_attn = n_heads * HEAD_DIM
    sample_state = (qbd_ref, bias_scr, sc_ref, sacc_ref)

    slot = step % 2
    copies = functools.partial(
        _page_copies, pt_ref, (ktc_ref, vtc_ref), (kbuf, vbuf), sem, layer=layer,
        n_pages=n_pages, pages_per_step=pages_per_step, chunks_per_seq=chunks_per_seq)

    @pl.when(step == 0)
    def _():
        for cp in copies(step=step, slot=slot, lookup=True):
            cp.start()

    @pl.when(step + 1 < n_steps)
    def _():
        for cp in copies(step=step + 1, slot=1 - slot, lookup=True):
            cp.start()

    @pl.when(chunk == 0)
    def _():
        _sample_init(bias_ref, sq_ref, *sample_state, n_heads=n_heads, t_new=t_new)
        rows_all = n_heads * t_new
        t_row = lax.broadcasted_iota(jnp.int32, (rows_all, PAGE_SIZE), 0) % t_new
        s_col = lax.broadcasted_iota(jnp.int32, (rows_all, PAGE_SIZE), 1)
        _sample_visit([knt_ref[0]], [vnt_ref[0]], s_col < t_row, *sample_state)

    def past_pages():
        for cp in copies(step=step, slot=slot, lookup=False):
            cp.wait()
        for lo in range(0, pages_per_step, SAMPLE_PAGES_PER_VISIT):
            pages = range(lo, lo + SAMPLE_PAGES_PER_VISIT)
            _sample_visit([kbuf[slot, p].astype(BF16) for p in pages],
                          [vbuf[slot, p].astype(BF16) for p in pages], None, *sample_state)

    _prompt_part(bias_ref, q_ref, kt_ref, vt_ref, o_ref, c_ref, acc_ref, hp=hp, i=i, blk=blk,
                 heads_per_step=heads_per_step, between=past_pages)

    @pl.when(chunk == chunks_per_seq - 1)
    def _():
        acc = sacc_ref[...]
        lane = lax.broadcasted_iota(jnp.int32, (t_new, d_attn), 1) // HEAD_DIM
        out = jnp.zeros((t_new, d_attn), F32)
        for h in range(n_heads):
            out = jnp.where(lane == h, acc[h * t_new:(h + 1) * t_new, :], out)
        so_ref[0] = out


def _attention(q, kt, vt, sq, knt, vnt, bias, kt_cache, vt_cache, layer, page_table,
               blk=ATTN_BLOCK, heads_per_step=ATTN_HEADS_PER_STEP):
    b, t, d_attn = q.shape
    bs, t_new, _ = sq.shape
    n_heads = d_attn // HEAD_DIM
    n_pages = page_table.shape[1]
    width = heads_per_step * HEAD_DIM
    grid = (b, d_attn // width, t // blk)
    n_steps = grid[0] * grid[1] * grid[2]
    chunks_per_seq, rem = divmod(n_steps, bs)
    pages_per_step, rem2 = divmod(n_pages, chunks_per_seq)
    assert rem == 0 and rem2 == 0 and pages_per_step % SAMPLE_PAGES_PER_VISIT == 0
    rows_all = n_heads * t_new

    def step_of(bi, hp, i):
        return (bi * grid[1] + hp) * grid[2] + i

    per_seq = lambda rows, cols: pl.BlockSpec(
        (1, rows, cols), lambda bi, hp, i, pt: (step_of(bi, hp, i) // chunks_per_seq, 0, 0))
    page_buf = pltpu.VMEM((2, pages_per_step, d_attn, PAGE_SIZE), F32)
    return pl.pallas_call(
        functools.partial(_attn_kernel, blk=blk, heads_per_step=heads_per_step,
                          pages_per_step=pages_per_step, chunks_per_seq=chunks_per_seq,
                          n_heads=n_heads, t_new=t_new, layer=layer, n_pages=n_pages),
        grid_spec=pltpu.PrefetchScalarGridSpec(
            num_scalar_prefetch=1,
            grid=grid,
            in_specs=[
                pl.BlockSpec(memory_space=pltpu.SMEM),
                pl.BlockSpec((1, blk, width), lambda bi, hp, i, pt: (bi, i, hp)),
                pl.BlockSpec((1, width, t), lambda bi, hp, i, pt: (bi, hp, 0)),
                pl.BlockSpec((1, width, t), lambda bi, hp, i, pt: (bi, hp, 0)),
                per_seq(t_new, d_attn), per_seq(d_attn, PAGE_SIZE), per_seq(d_attn, PAGE_SIZE),
                pl.BlockSpec(memory_space=pl.ANY), pl.BlockSpec(memory_space=pl.ANY),
            ],
            out_specs=[pl.BlockSpec((1, blk, width), lambda bi, hp, i, pt: (bi, i, hp)),
                       per_seq(t_new, d_attn)],
            scratch_shapes=[
                pltpu.VMEM((heads_per_step, blk, 128), F32),
                pltpu.VMEM((heads_per_step, blk, HEAD_DIM), F32),
                pltpu.VMEM((rows_all, d_attn), BF16),
                pltpu.VMEM((rows_all, 1), F32),
                pltpu.VMEM((rows_all, 1), F32),
                pltpu.VMEM((rows_all, d_attn), F32),
                page_buf, page_buf,
                pltpu.SemaphoreType.DMA((2, 2)),
            ],
        ),
        out_shape=[jax.ShapeDtypeStruct((b, t, d_attn), BF16),
                   jax.ShapeDtypeStruct((bs, t_new, d_attn), F32)],
        compiler_params=_params("arbitrary", "arbitrary", "arbitrary"),
        name="sb_attn",
    )(page_table.reshape(-1), bias, q, kt, vt, sq, knt, vnt, kt_cache, vt_cache)


def _silu(x):
    return x * (1.0 / (1.0 + jnp.exp(-x)))


def _mix_out_kernel(x_ref, a_ref, ga_ref, u_ref, uh_ref, pre_ref, gb_ref, wp_ref, ps_ref,
                    wo_ref, fg_ref, o_ref, seq_ref, *, tt, pos0, final):
    nb = x_ref.shape[0]
    rows = nb * tt
    j = pl.program_id(1)
    seq_ref[:, 0:HALO, :] = jnp.where(j == 0, pre_ref[...], uh_ref[...])
    seq_ref[:, HALO:, :] = u_ref[...]
    pos = pos0 + j * tt + lax.broadcasted_iota(jnp.int32, (tt, 1), 0)
    flat = lambda ref: ref[...].reshape(rows, ref.shape[-1]).astype(F32)
    gb = flat(gb_ref)
    parts = [flat(a_ref) * _silu(flat(ga_ref))]
    for g, w in enumerate(POOL_WINDOWS):
        lanes = slice(g * POOL_GROUP_DIM, (g + 1) * POOL_GROUP_DIM)
        count = jnp.minimum(pos + 1, w).astype(F32)
        diffs = []
        for s in range(nb):
            win_sum = seq_ref[s, HALO:HALO + tt, lanes]
            for back in range(1, w):
                win_sum = win_sum + seq_ref[s, HALO - back:HALO - back + tt, lanes]
            diffs.append(win_sum / count - seq_ref[s, HALO:HALO + tt, lanes])
        diff = jnp.concatenate(diffs, axis=0)
        y = jnp.dot(diff.astype(BF16), wp_ref[g], preferred_element_type=F32) * ps_ref[:, lanes]
        parts.append(y * _silu(gb[:, lanes]))
    mix = jnp.concatenate(parts, axis=-1).astype(BF16)
    h = flat(x_ref) + jnp.dot(mix, wo_ref[...], preferred_element_type=F32)
    if final:
        ms = jnp.mean(h * h, axis=-1, keepdims=True)
        h = h * lax.rsqrt(ms + RMS_EPS) * fg_ref[...]
    o_ref[...] = h.reshape(o_ref.shape)


def _mix_out(x, a, gate_a, u, prefix, gate_b, w_pool, pool_scale, w_out, final_g, *, nb, tt, pos0,
             final):
    b, t, d = x.shape
    d_attn = a.shape[-1]
    d_pool = u.shape[-1]
    tile = lambda width: pl.BlockSpec((nb, tt, width), lambda bi, j: (bi, j, 0))
    const = lambda shape: pl.BlockSpec(shape, lambda bi, j: (0,) * len(shape))
    halo_blocks = tt // HALO if tt >= HALO else 1
    halo_spec = pl.BlockSpec(
        (nb, HALO, d_pool), lambda bi, j: (bi, jnp.maximum(j * halo_blocks - 1, 0), 0))
    u_halo = u if t >= HALO else prefix
    return pl.pallas_call(
        functools.partial(_mix_out_kernel, tt=tt, pos0=pos0, final=final),
        grid=(b // nb, t // tt),
        in_specs=[
            tile(d), tile(d_attn), tile(d_attn), tile(d_pool), halo_spec,
            pl.BlockSpec((nb, HALO, d_pool), lambda bi, j: (bi, 0, 0)),
            tile(d_pool),
            const(w_pool.shape), const((1, d_pool)), const(w_out.shape), const((1, d)),
        ],
        out_specs=tile(d),
        out_shape=jax.ShapeDtypeStruct((b, t, d), F32),
        scratch_shapes=[pltpu.VMEM((nb, HALO + tt, d_pool), F32)],
        compiler_params=_params("parallel", "arbitrary"),
        name="mix_out",
    )(x, a, gate_a, u, u_halo, prefix, gate_b, w_pool, pool_scale.reshape(1, d_pool), w_out,
      final_g.reshape(1, d))


def kernel(x_prompt, x_sample, cache_k, cache_v, state_pool, page_table, norm_g, w_in, sb_bias,
           w_pool, pool_scale, w_out, final_g):
    depth = norm_g.shape[0]
    bp, tp, d_model = x_prompt.shape
    bs, ts, _ = x_sample.shape
    n_heads = cache_k.shape[3]
    d_attn = n_heads * HEAD_DIM
    d_pool = state_pool.shape[-1]
    n_pool = cache_k.shape[1]
    past = page_table.shape[1] * PAGE_SIZE

    w_in_b = w_in.astype(BF16)
    wkvt_b = jnp.swapaxes(w_in_b[:, :, d_attn:3 * d_attn], 1, 2)
    w_pool_b = w_pool.astype(BF16)
    w_out_b = w_out.astype(BF16)
    kt_cache = cache_k.transpose(0, 1, 3, 4, 2).reshape(depth, n_pool, d_attn, PAGE_SIZE)
    vt_cache = cache_v.transpose(0, 1, 3, 4, 2).reshape(depth, n_pool, d_attn, PAGE_SIZE)

    h_p = x_prompt
    h_s = x_sample.reshape(1, bs * ts, d_model)
    zero_prefix = jnp.zeros((bp, HALO, d_pool), F32)
    per_seq = lambda z: z.reshape(bs, ts, -1)
    kv_prompt = None
    pool_prompt, k_sample, v_sample, pool_sample = [], [], [], []
    for l in range(depth):
        final = l == depth - 1
        q, kt, vt, ktb, vtb, ga, u, gb = _norm_proj(
            h_p, norm_g[l], w_in_b[l], wkvt_b[l], d_attn, d_pool, tm=ROW_TILE, prompt=True,
            layer=l, depth=depth, kv_prev=kv_prompt)
        kv_prompt = (kt, vt)
        sq, kf, vf, sktb, svtb, sga, su, sgb = _norm_proj(
            h_s, norm_g[l], w_in_b[l], wkvt_b[l], d_attn, d_pool, tm=bs * ts, prompt=False)

        def new_page(zt):
            zt = zt.reshape(d_attn, bs, ts).transpose(1, 0, 2)
            return jnp.pad(zt, ((0, 0), (0, 0), (0, PAGE_SIZE - ts)))

        a_p, a_s = _attention(q, ktb, vtb, per_seq(sq), new_page(sktb), new_page(svtb), sb_bias[l],
                              kt_cache, vt_cache, l, page_table)
        h_p = _mix_out(h_p, a_p, ga, u, zero_prefix, gb, w_pool_b[l], pool_scale[l], w_out_b[l],
                       final_g, nb=1, tt=ROW_TILE, pos0=0, final=final)
        prefix = jnp.pad(state_pool[l], ((0, 0), (HALO - POOL_STATE_LEN, 0), (0, 0)))
        h_s = _mix_out(per_seq(h_s), a_s, per_seq(sga), per_seq(su), prefix, per_seq(sgb),
                       w_pool_b[l], pool_scale[l], w_out_b[l], final_g, nb=bs, tt=ts, pos0=past,
                       final=final).reshape(1, bs * ts, d_model)
        pool_prompt.append(u[:, tp - POOL_STATE_LEN:])
        k_sample.append(kf.reshape(bs, ts, n_heads, HEAD_DIM))
        v_sample.append(vf.reshape(bs, ts, n_heads, HEAD_DIM))
        pool_sample.append(
            jnp.concatenate([state_pool[l], per_seq(su)], axis=1)[:, -POOL_STATE_LEN:])

    def token_major(zt):
        return zt.reshape(depth, bp, n_heads, HEAD_DIM, tp).transpose(0, 1, 4, 2, 3)

    return (h_p, h_s.reshape(bs, ts, d_model), token_major(kv_prompt[0]), token_major(kv_prompt[1]),
            jnp.stack(pool_prompt), jnp.stack(k_sample), jnp.stack(v_sample), jnp.stack(pool_sample))
```
